```python
import math
import jax, jax.numpy as jnp
from jax import lax
import numpy as np

D_MODEL = 2048
BATCH = 8
SEQ = 4096
DEPTH = 1
DEC_BATCH = 8
DEC_SEQ = 16
PAST_LEN = 4096

CHUNK = 64
Q_BLOCK = 128
DA_WIDTH = D_MODEL // 2
DA_HEAD_DIM = 64
DA_HEADS = DA_WIDTH // (2 * DA_HEAD_DIM)
ML_WIDTH = D_MODEL - DA_WIDTH
ML_HEADS = 4
ML_HEAD_DIM = ML_WIDTH // ML_HEADS
MEM_LEN = 256
MEM_HEADS = 4
MEM_HEAD_DIM = D_MODEL // MEM_HEADS
D_FF = 5504
CONV_W = 3
EPS = 1e-6
NEG = -1e30
IN_SIZES = (DA_WIDTH, DA_WIDTH, DA_WIDTH, ML_WIDTH, ML_WIDTH, ML_WIDTH, ML_WIDTH, ML_HEADS, ML_HEADS)
D_IN = sum(IN_SIZES)
IN_SPLITS = tuple(sum(IN_SIZES[:i + 1]) for i in range(len(IN_SIZES) - 1))

kernel_name = "hymba_diffattn_mlstm_streaming_step"

F32 = jnp.float32


def _rmsnorm(x, g):
    xf = x.astype(F32)
    y = xf * lax.rsqrt(jnp.mean(xf * xf, axis=-1, keepdims=True) + EPS)
    return (y * g.astype(F32)).astype(x.dtype)


def _diff_attn_core(q, k, v, q_pos, k_pos, lam):
    s = jnp.einsum('bqhcd,bkhcd->bhcqk', q, k).astype(F32) * (DA_HEAD_DIM ** -0.5)
    mask = (k_pos[None, :] // CHUNK) <= (q_pos[:, None] // CHUNK)
    s = jnp.where(mask, s, NEG)
    p = jax.nn.softmax(s, axis=-1)
    a = p[:, :, 0] - lam * p[:, :, 1]
    return jnp.einsum('bhqk,bkhe->bqhe', a.astype(v.dtype), v)


def _diff_attn_prompt(q, k, v, lam):
    B, T = q.shape[0], q.shape[1]
    nb = T // Q_BLOCK
    qb = jnp.moveaxis(q.reshape(B, nb, Q_BLOCK, DA_HEADS, 2, DA_HEAD_DIM), 1, 0)
    k_pos = jnp.arange(T)

    def one(args):
        qi, bi = args
        q_pos = bi * Q_BLOCK + jnp.arange(Q_BLOCK)
        return _diff_attn_core(qi, k, v, q_pos, k_pos, lam)

    ob = lax.map(one, (qb, jnp.arange(nb)))
    return jnp.moveaxis(ob, 0, 1).reshape(B, T, DA_HEADS, 2 * DA_HEAD_DIM)


def _mlstm_chunkwise(q, k, v, ig, lf, c0, n0, m0, chunk):
    B, H, T, d = q.shape
    nc = T // chunk

    def to_chunks(a):
        return jnp.moveaxis(a.reshape((B, H, nc, chunk) + a.shape[3:]), 2, 0)

    causal = jnp.tril(jnp.ones((chunk, chunk), dtype=bool))

    def step(carry, inp):
        c, n, m = carry
        qc, kc, vc, ic, fc = inp
        b = jnp.cumsum(fc, axis=-1)
        dmat = b[..., :, None] - b[..., None, :] + ic[..., None, :]
        dmat = jnp.where(causal, dmat, -jnp.inf)
        inter = b + m[..., None]
        m_t = jnp.maximum(inter, jnp.max(dmat, axis=-1))
        w = jnp.exp(dmat - m_t[..., None])
        g = jnp.exp(inter - m_t)
        s = jnp.einsum('bhtd,bhsd->bhts', qc, kc) * w
        num = jnp.einsum('bhts,bhse->bhte', s, vc) + g[..., None] * jnp.einsum('bhed,bhtd->bhte', c, qc)
        den = jnp.sum(s, axis=-1) + g * jnp.einsum('bhd,bhtd->bht', n, qc)
        h = num / jnp.maximum(jnp.abs(den), jnp.exp(-m_t))[..., None]
        m_new = m_t[..., -1]
        wl = jnp.exp(b[..., -1:] - b + ic - m_new[..., None])
        gl = jnp.exp(inter[..., -1] - m_new)
        c_new = gl[..., None, None] * c + jnp.einsum('bhs,bhse,bhsd->bhed', wl, vc, kc)
        n_new = gl[..., None] * n + jnp.einsum('bhs,bhsd->bhd', wl, kc)
        return (c_new, n_new, m_new), h

    (c, n, m), hs = lax.scan(step, (c0, n0, m0),
                             (to_chunks(q), to_chunks(k), to_chunks(v), to_chunks(ig), to_chunks(lf)))
    h = jnp.moveaxis(hs, 0, 2).reshape(B, H, T, d)
    return h, c, n, m


def _token_mixer(h, p, past_k, past_v, ml_state, lam_init, ml_chunk):
    B, T, _ = h.shape
    dq, dk, dv, mq, mk, mv, mo, mi, mf = jnp.split(h @ p['w_in'], IN_SPLITS, axis=-1)
    lam = (jnp.exp(jnp.sum((p['lq1'] * p['lk1']).astype(F32)))
           - jnp.exp(jnp.sum((p['lq2'] * p['lk2']).astype(F32))) + lam_init)
    q = dq.reshape(B, T, DA_HEADS, 2, DA_HEAD_DIM)
    k_rows = dk.reshape(B, T, DA_HEADS, 2 * DA_HEAD_DIM)
    v_rows = dv.reshape(B, T, DA_HEADS, 2 * DA_HEAD_DIM)
    if past_k is None:
        o = _diff_attn_prompt(q, k_rows.reshape(B, T, DA_HEADS, 2, DA_HEAD_DIM), v_rows, lam)
    else:
        P = past_k.shape[1]
        keys = jnp.concatenate([past_k.astype(k_rows.dtype), k_rows], axis=1)
        vals = jnp.concatenate([past_v.astype(v_rows.dtype), v_rows], axis=1)
        o = _diff_attn_core(q, keys.reshape(B, P + T, DA_HEADS, 2, DA_HEAD_DIM), vals,
                            P + jnp.arange(T), jnp.arange(P + T), lam)
    o = _rmsnorm(o, p['g_da']) * (1.0 - lam_init)
    da_out = o.reshape(B, T, DA_WIDTH)
    def heads(a):
        return a.reshape(B, T, ML_HEADS, ML_HEAD_DIM).transpose(0, 2, 1, 3).astype(F32)
    qm = heads(mq)
    km = heads(mk) * (ML_HEAD_DIM ** -0.5)
    vm = heads(mv)
    ig = (mi + p['b_ig']).astype(F32).transpose(0, 2, 1)
    lf = jax.nn.log_sigmoid((mf + p['b_fg']).astype(F32)).transpose(0, 2, 1)
    c0, n0, m0 = ml_state
    hm, c, n, m = _mlstm_chunkwise(qm, km, vm, ig, lf, c0.astype(F32), n0.astype(F32),
                                   m0.astype(F32), ml_chunk)
    hm = hm.transpose(0, 2, 1, 3).astype(h.dtype)
    hm = _rmsnorm(hm, p['g_ml'].reshape(ML_HEADS, ML_HEAD_DIM))
    ml_out = hm.reshape(B, T, ML_WIDTH) * jax.nn.sigmoid(mo)
    y = jnp.concatenate([da_out, ml_out], axis=-1) @ p['w_out']
    return y, k_rows, v_rows, (c, n, m)


def _mem_kv(mem, g_mem, w_mk, w_mv):
    B, M, _ = mem.shape
    mn = _rmsnorm(mem, g_mem)
    return ((mn @ w_mk).reshape(B, M, MEM_HEADS, MEM_HEAD_DIM),
            (mn @ w_mv).reshape(B, M, MEM_HEADS, MEM_HEAD_DIM))


def _mem_attn(h, mem_k, mem_v, w_mq, w_mo):
    B, T, _ = h.shape
    q = (h @ w_mq).reshape(B, T, MEM_HEADS, MEM_HEAD_DIM)
    s = jnp.einsum('bqhd,bkhd->bhqk', q, mem_k.astype(q.dtype)).astype(F32) * (MEM_HEAD_DIM ** -0.5)
    a = jax.nn.softmax(s, axis=-1).astype(q.dtype)
    o = jnp.einsum('bhqk,bkhd->bqhd', a, mem_v.astype(q.dtype)).reshape(B, T, D_MODEL)
    return o @ w_mo


def _conv_ffn(h, p, conv_past):
    B, T, _ = h.shape
    g = h @ p['w_gate']
    u = h @ p['w_up']
    if conv_past is None:
        past = jnp.zeros((B, CONV_W - 1, D_FF), g.dtype)
    else:
        past = conv_past.astype(g.dtype)
    gp = jnp.concatenate([past, g], axis=1)
    c = p['conv_b']
    for j in range(CONV_W):
        c = c + p['conv_w'][j] * gp[:, j:j + T]
    out = (jax.nn.silu(c) * u) @ p['w_down']
    return out, gp[:, T:]


def _layer(x, mem_k, mem_v, p, past_k, past_v, ml_state, conv_past, lam_init, ml_chunk):
    y, k_rows, v_rows, ml_new = _token_mixer(_rmsnorm(x, p['g_mix']), p, past_k, past_v,
                                             ml_state, lam_init, ml_chunk)
    x = x + y
    x = x + _mem_attn(_rmsnorm(x, p['g_xattn']), mem_k, mem_v, p['w_mq'], p['w_mo'])
    f, conv_new = _conv_ffn(_rmsnorm(x, p['g_ffn']), p, conv_past)
    x = x + f
    return x, k_rows, v_rows, ml_new, conv_new


def setup_inputs(seed: int = 0) -> dict:
    key = jax.random.key(seed)
    ks = iter(jax.random.split(key, 40))

    def nrm(shape, scale=1.0):
        return jax.random.normal(next(ks), shape, F32) * scale

    def gain(shape):
        return 1.0 + nrm(shape, 0.05)

    return {
        'x_prompt': nrm((BATCH, SEQ, D_MODEL)),
        'x_sample': nrm((DEC_BATCH, DEC_SEQ, D_MODEL)),
        'cache_da_k': nrm((DEPTH, DEC_BATCH, PAST_LEN, DA_HEADS, 2 * DA_HEAD_DIM)),
        'cache_da_v': nrm((DEPTH, DEC_BATCH, PAST_LEN, DA_HEADS, 2 * DA_HEAD_DIM)),
        'state_ml_c': nrm((DEPTH, DEC_BATCH, ML_HEADS, ML_HEAD_DIM, ML_HEAD_DIM), 0.1),
        'state_ml_n': nrm((DEPTH, DEC_BATCH, ML_HEADS, ML_HEAD_DIM), 0.1),
        'state_ml_m': nrm((DEPTH, DEC_BATCH, ML_HEADS)),
        'state_ffn_conv': nrm((DEPTH, DEC_BATCH, CONV_W - 1, D_FF)),
        'cache_mem_k': nrm((DEPTH, DEC_BATCH, MEM_LEN, MEM_HEADS, MEM_HEAD_DIM)),
        'cache_mem_v': nrm((DEPTH, DEC_BATCH, MEM_LEN, MEM_HEADS, MEM_HEAD_DIM)),
        'mem_prompt': nrm((BATCH, MEM_LEN, D_MODEL)),
        'w_in': nrm((DEPTH, D_MODEL, D_IN), D_MODEL ** -0.5),
        'g_mix': gain((DEPTH, D_MODEL)),
        'lambda_q1': nrm((DEPTH, DA_HEAD_DIM), 0.1),
        'lambda_k1': nrm((DEPTH, DA_HEAD_DIM), 0.1),
        'lambda_q2': nrm((DEPTH, DA_HEAD_DIM), 0.1),
        'lambda_k2': nrm((DEPTH, DA_HEAD_DIM), 0.1),
        'g_da_sub': gain((DEPTH, 2 * DA_HEAD_DIM)),
        'b_ig': nrm((DEPTH, ML_HEADS), 0.1),
        'b_fg': jnp.linspace(3.0, 6.0, ML_HEADS, dtype=F32)[None, :] + nrm((DEPTH, ML_HEADS), 0.1),
        'g_ml': gain((DEPTH, ML_WIDTH)),
        'w_out': nrm((DEPTH, D_MODEL, D_MODEL), D_MODEL ** -0.5),
        'g_xattn': gain((DEPTH, D_MODEL)),
        'g_mem': gain((DEPTH, D_MODEL)),
        'w_mq': nrm((DEPTH, D_MODEL, D_MODEL), D_MODEL ** -0.5),
        'w_mk': nrm((DEPTH, D_MODEL, D_MODEL), D_MODEL ** -0.5),
        'w_mv': nrm((DEPTH, D_MODEL, D_MODEL), D_MODEL ** -0.5),
        'w_mo': nrm((DEPTH, D_MODEL, D_MODEL), D_MODEL ** -0.5),
        'g_ffn': gain((DEPTH, D_MODEL)),
        'w_gate': nrm((DEPTH, D_MODEL, D_FF), D_MODEL ** -0.5),
        'w_up': nrm((DEPTH, D_MODEL, D_FF), D_MODEL ** -0.5),
        'conv_w': nrm((DEPTH, CONV_W, D_FF), CONV_W ** -0.5),
        'conv_b': nrm((DEPTH, D_FF), 0.02),
        'w_down': nrm((DEPTH, D_FF, D_MODEL), D_FF ** -0.5),
        'g_final': gain((D_MODEL,)),
    }


def reference(x_prompt, x_sample, cache_da_k, cache_da_v, state_ml_c, state_ml_n, state_ml_m,
              state_ffn_conv, cache_mem_k, cache_mem_v, mem_prompt, w_in, g_mix, lambda_q1,
              lambda_k1, lambda_q2, lambda_k2, g_da_sub, b_ig, b_fg, g_ml, w_out, g_xattn, g_mem,
              w_mq, w_mk, w_mv, w_mo, g_ffn, w_gate, w_up, conv_w, conv_b, w_down, g_final):
    xp = x_prompt
    xs = x_sample
    B = xp.shape[0]
    p_k, p_v, p_c, p_n, p_m, p_conv, p_mk, p_mv = [], [], [], [], [], [], [], []
    s_k, s_v, s_c, s_n, s_m, s_conv = [], [], [], [], [], []
    for l in range(DEPTH):
        p = dict(w_in=w_in[l], g_mix=g_mix[l], lq1=lambda_q1[l], lk1=lambda_k1[l],
                 lq2=lambda_q2[l], lk2=lambda_k2[l], g_da=g_da_sub[l], b_ig=b_ig[l],
                 b_fg=b_fg[l], g_ml=g_ml[l], w_out=w_out[l], g_xattn=g_xattn[l],
                 w_mq=w_mq[l], w_mo=w_mo[l], g_ffn=g_ffn[l], w_gate=w_gate[l], w_up=w_up[l],
                 conv_w=conv_w[l], conv_b=conv_b[l], w_down=w_down[l])
        lam_init = 0.8 - 0.6 * math.exp(-0.3 * l)
        mk, mv = _mem_kv(mem_prompt, g_mem[l], w_mk[l], w_mv[l])
        zero_state = (jnp.zeros((B, ML_HEADS, ML_HEAD_DIM, ML_HEAD_DIM), F32),
                      jnp.zeros((B, ML_HEADS, ML_HEAD_DIM), F32),
                      jnp.zeros((B, ML_HEADS), F32))
        xp, kr, vr, (c, n, m), cv = _layer(xp, mk, mv, p, None, None, zero_state, None,
                                           lam_init, CHUNK)
        p_k.append(kr); p_v.append(vr); p_c.append(c); p_n.append(n); p_m.append(m)
        p_conv.append(cv); p_mk.append(mk); p_mv.append(mv)
        xs, kr2, vr2, (c2, n2, m2), cv2 = _layer(
            xs, cache_mem_k[l], cache_mem_v[l], p, cache_da_k[l], cache_da_v[l],
            (state_ml_c[l], state_ml_n[l], state_ml_m[l]), state_ffn_conv[l], lam_init,
            xs.shape[1])
        s_k.append(kr2); s_v.append(vr2); s_c.append(c2); s_n.append(n2); s_m.append(m2)
        s_conv.append(cv2)
    y_prompt = _rmsnorm(xp, g_final)
    y_sample = _rmsnorm(xs, g_final)
    return (y_prompt, y_sample,
            jnp.stack(p_k), jnp.stack(p_v), jnp.stack(p_c), jnp.stack(p_n), jnp.stack(p_m),
            jnp.stack(p_conv), jnp.stack(p_mk), jnp.stack(p_mv),
            jnp.stack(s_k), jnp.stack(s_v), jnp.stack(s_c), jnp.stack(s_n), jnp.stack(s_m),
            jnp.stack(s_conv))
```

```python
import functools
import math

import jax
import jax.numpy as jnp
from jax import lax
from jax.experimental import pallas as pl
from jax.experimental.pallas import tpu as pltpu

F32 = jnp.float32
BF16 = jnp.bfloat16

EPS = 1e-6
NEG = -1e30
CHUNK = 64
DA_HEADS = 8
DA_HEAD_DIM = 64
ML_HEADS = 4
MEM_HEADS = 4
CONV_W = 3

LANES = 128
V7X_VMEM_BYTES = 64 * 1024 * 1024
VMEM_CEILING = V7X_VMEM_BYTES - 8 * 1024 * 1024


def _vmem_limit(block_bytes):
    return int(min(VMEM_CEILING, block_bytes * 1.25 + 8 * 1024 * 1024))


def _nbytes(shape, dtype):
    return math.prod(shape) * jnp.dtype(dtype).itemsize


def _rms(x, g):
    return x * lax.rsqrt(jnp.mean(x * x, axis=-1, keepdims=True) + EPS) * g


def _proj_kernel(*refs, n_pieces, out_plan, has_gate):
    x_ref, g_ref = refs[0], refs[1]
    w_refs = refs[2:2 + n_pieces]
    pos = 2 + n_pieces
    if has_gate:
        wgh_ref, wgl_ref = refs[pos], refs[pos + 1]
        pos += 2
    out_refs = refs[pos:pos + len(out_plan)]
    pos += len(out_plan)
    if has_gate:
        gate_ref = refs[pos]
        pos += 1
    xn_ref = refs[pos]

    @pl.when(pl.program_id(1) == 0)
    def _():
        xn = _rms(x_ref[...], g_ref[...])
        hi = xn.astype(BF16)
        xn_ref[...] = hi
        if has_gate:
            lo = (xn - hi.astype(F32)).astype(BF16)
            gate_ref[...] = (jnp.dot(hi, wgh_ref[...], preferred_element_type=F32)
                             + jnp.dot(lo, wgh_ref[...], preferred_element_type=F32)
                             + jnp.dot(hi, wgl_ref[...], preferred_element_type=F32))

    xn = xn_ref[...]
    for p in range(n_pieces):
        acc = jnp.dot(xn, w_refs[p][...], preferred_element_type=F32)
        for (piece, _), o_ref in zip(out_plan, out_refs):
            if piece == p:
                o_ref[...] = acc.astype(o_ref.dtype)


def _norm_proj(x, g, w, n_pieces, width, out_plan, tm, tn, gate_w=None, name="proj"):
    n, d = x.shape
    nj = width // tn
    in_specs = [pl.BlockSpec((tm, d), lambda i, j: (i, 0)),
                pl.BlockSpec((1, d), lambda i, j: (0, 0))]
    args = [x, g.reshape(1, d)]
    for p in range(n_pieces):
        in_specs.append(pl.BlockSpec((d, tn), functools.partial(lambda i, j, p: (0, p * nj + j), p=p)))
        args.append(w)
    has_gate = gate_w is not None
    if has_gate:
        for gw in gate_w:
            in_specs.append(pl.BlockSpec((d, LANES), lambda i, j: (0, 0)))
            args.append(gw)
    out_shape = [jax.ShapeDtypeStruct((n, width), dt) for _, dt in out_plan]
    out_specs = [pl.BlockSpec((tm, tn), lambda i, j: (i, j)) for _ in out_plan]
    if has_gate:
        out_shape.append(jax.ShapeDtypeStruct((n, LANES), F32))
        out_specs.append(pl.BlockSpec((tm, LANES), lambda i, j: (i, 0)))
    resident = (2 * _nbytes((tm, d), F32) + _nbytes((tm, d), BF16)
                + 2 * n_pieces * _nbytes((d, tn), BF16)
                + 2 * sum(_nbytes((tm, tn), dt) for _, dt in out_plan)
                + (4 * _nbytes((d, LANES), BF16) + 2 * _nbytes((tm, LANES), F32) if has_gate else 0)
                + 2 * _nbytes((tm, tn), F32))
    return pl.pallas_call(
        functools.partial(_proj_kernel, n_pieces=n_pieces, out_plan=tuple(out_plan), has_gate=has_gate),
        grid=(n // tm, nj),
        in_specs=in_specs,
        out_specs=out_specs,
        out_shape=out_shape,
        scratch_shapes=[pltpu.VMEM((tm, d), BF16)],
        compiler_params=pltpu.CompilerParams(
            dimension_semantics=("arbitrary", "arbitrary"),
            vmem_limit_bytes=_vmem_limit(resident)),
        name=name,
    )(*args)


def _da_update(q, k, v, mask, m_ref, l_ref, acc_ref, h):
    lane = lax.broadcasted_iota(jnp.int32, k.shape, 1)
    zero = jnp.zeros_like(k)
    for c in range(2):
        kc = jnp.where((lane < DA_HEAD_DIM) if c == 0 else (lane >= DA_HEAD_DIM), k, zero)
        s = lax.dot_general(q, kc, (((1,), (1,)), ((), ())), preferred_element_type=F32)
        if mask is not None:
            s = jnp.where(mask, s, NEG)
        idx = 2 * h + c
        m_prev = m_ref[idx]
        m_new = jnp.maximum(m_prev, jnp.max(s, axis=-1, keepdims=True))
        p = jnp.exp(s - m_new)
        alpha = jnp.exp(m_prev - m_new)
        l_ref[idx] = alpha * l_ref[idx] + jnp.sum(p, axis=-1, keepdims=True)
        m_ref[idx] = m_new
        pv = jnp.dot(p.astype(BF16), v, preferred_element_type=F32)
        sl = slice(idx * LANES, (idx + 1) * LANES)
        acc_ref[:, sl] = alpha * acc_ref[:, sl] + pv


def _da_init(m_ref, l_ref, acc_ref):
    m_ref[...] = jnp.full(m_ref.shape, NEG, F32)
    l_ref[...] = jnp.zeros(l_ref.shape, F32)
    acc_ref[...] = jnp.zeros(acc_ref.shape, F32)


def _da_finalize(lams_ref, gda_ref, m_ref, l_ref, acc_ref, o_ref, lam_init):
    lams = lams_ref[...]
    lam = (jnp.exp(jnp.sum(lams[0:1] * lams[1:2], axis=-1, keepdims=True))
           - jnp.exp(jnp.sum(lams[2:3] * lams[3:4], axis=-1, keepdims=True)) + lam_init)
    gda = gda_ref[...]
    for h in range(DA_HEADS):
        o0 = acc_ref[:, (2 * h) * LANES:(2 * h + 1) * LANES] / l_ref[2 * h]
        o1 = acc_ref[:, (2 * h + 1) * LANES:(2 * h + 2) * LANES] / l_ref[2 * h + 1]
        o = o0 - lam * o1
        o_ref[:, h * LANES:(h + 1) * LANES] = (_rms(o, gda) * (1.0 - lam_init)).astype(o_ref.dtype)


def _chunk_mask(q0, k0, tq, tk):
    q_pos = q0 + lax.broadcasted_iota(jnp.int32, (tq, tk), 0)
    k_pos = k0 + lax.broadcasted_iota(jnp.int32, (tq, tk), 1)
    return (k_pos // CHUNK) <= (q_pos // CHUNK)


def _da_prompt_kernel(lams_ref, gda_ref, q_ref, k_ref, v_ref, o_ref, m_ref, l_ref, acc_ref, *, tq, lam_init):
    qi, ki = pl.program_id(1), pl.program_id(2)

    @pl.when(ki == 0)
    def _():
        _da_init(m_ref, l_ref, acc_ref)

    @pl.when(ki <= qi)
    def _():
        mask = _chunk_mask(qi * tq, ki * tq, tq, tq)
        for h in range(DA_HEADS):
            sl = slice(h * LANES, (h + 1) * LANES)
            q = q_ref[:, sl] * (DA_HEAD_DIM ** -0.5)
            _da_update(q, k_ref[:, sl], v_ref[:, sl], mask, m_ref, l_ref, acc_ref, h)

    @pl.when(ki == qi)
    def _():
        _da_finalize(lams_ref, gda_ref, m_ref, l_ref, acc_ref, o_ref, lam_init)


def _da_prompt(q, k, v, lams, g_da, batch, seq, lam_init, tq):
    n, w = q.shape
    nq = seq // tq
    qmap = lambda b, qi, ki: (b * nq + qi, 0)
    kmap = lambda b, qi, ki: (b * nq + jnp.minimum(ki, qi), 0)
    const = lambda b, qi, ki: (0, 0)
    resident = (8 * _nbytes((tq, w), BF16) + _nbytes((tq, 2 * w), F32)
                + 2 * 2 * DA_HEADS * _nbytes((tq, LANES), F32) + 6 * _nbytes((tq, tq), F32))
    return pl.pallas_call(
        functools.partial(_da_prompt_kernel, tq=tq, lam_init=lam_init),
        grid=(batch, nq, nq),
        in_specs=[pl.BlockSpec(lams.shape, const), pl.BlockSpec(g_da.shape, const),
                  pl.BlockSpec((tq, w), qmap), pl.BlockSpec((tq, w), kmap), pl.BlockSpec((tq, w), kmap)],
        out_specs=pl.BlockSpec((tq, w), qmap),
        out_shape=jax.ShapeDtypeStruct((n, w), BF16),
        scratch_shapes=[pltpu.VMEM((2 * DA_HEADS, tq, 1), F32), pltpu.VMEM((2 * DA_HEADS, tq, 1), F32),
                        pltpu.VMEM((tq, 2 * w), F32)],
        compiler_params=pltpu.CompilerParams(
            dimension_semantics=("arbitrary", "arbitrary", "arbitrary"),
            vmem_limit_bytes=_vmem_limit(resident)),
        name="da_prompt",
    )(lams, g_da, q, k, v)


def _da_sample_kernel(lams_ref, gda_ref, q_ref, kn_ref, vn_ref, ck_ref, cv_ref, o_ref, m_ref, l_ref, acc_ref,
                      *, t, tk, past, lam_init):
    ki = pl.program_id(1)

    @pl.when(ki == 0)
    def _():
        _da_init(m_ref, l_ref, acc_ref)

    scale = DA_HEAD_DIM ** -0.5
    kblk = ck_ref[0].astype(BF16)
    vblk = cv_ref[0].astype(BF16)
    mask = _chunk_mask(past, ki * tk, t, tk)
    for h in range(DA_HEADS):
        sl = slice(h * LANES, (h + 1) * LANES)
        _da_update(q_ref[:, sl] * scale, kblk[:, sl], vblk[:, sl], mask, m_ref, l_ref, acc_ref, h)

    @pl.when(ki == pl.num_programs(1) - 1)
    def _():
        mask_new = _chunk_mask(past, past, t, t)
        for h in range(DA_HEADS):
            sl = slice(h * LANES, (h + 1) * LANES)
            _da_update(q_ref[:, sl] * scale, kn_ref[:, sl], vn_ref[:, sl], mask_new, m_ref, l_ref, acc_ref, h)
        _da_finalize(lams_ref, gda_ref, m_ref, l_ref, acc_ref, o_ref, lam_init)


def _da_sample(q, kn, vn, cache_k, cache_v, lams, g_da, lam_init, tk):
    n, w = q.shape
    batch, past, _ = cache_k.shape
    t = n // batch
    const = lambda b, ki: (0, 0)
    rows = lambda b, ki: (b, 0)
    cmap = lambda b, ki: (b, ki, 0)
    resident = (4 * _nbytes((tk, w), F32) + 2 * _nbytes((tk, w), BF16) + 8 * _nbytes((t, w), BF16)
                + _nbytes((t, 2 * w), F32) + 4 * DA_HEADS * _nbytes((t, LANES), F32))
    return pl.pallas_call(
        functools.partial(_da_sample_kernel, t=t, tk=tk, past=past, lam_init=lam_init),
        grid=(batch, past // tk),
        in_specs=[pl.BlockSpec(lams.shape, const), pl.BlockSpec(g_da.shape, const),
                  pl.BlockSpec((t, w), rows), pl.BlockSpec((t, w), rows), pl.BlockSpec((t, w), rows),
                  pl.BlockSpec((1, tk, w), cmap), pl.BlockSpec((1, tk, w), cmap)],
        out_specs=pl.BlockSpec((t, w), rows),
        out_shape=jax.ShapeDtypeStruct((n, w), BF16),
        scratch_shapes=[pltpu.VMEM((2 * DA_HEADS, t, 1), F32), pltpu.VMEM((2 * DA_HEADS, t, 1), F32),
                        pltpu.VMEM((t, 2 * w), F32)],
        compiler_params=pltpu.CompilerParams(
            dimension_semantics=("arbitrary", "arbitrary"),
            vmem_limit_bytes=_vmem_limit(resident)),
        name="da_sample",
    )(lams, g_da, q, kn, vn, cache_k, cache_v)


def _ml_kernel(q_ref, k_ref, v_ref, og_ref, gate_ref, bias_ref, gml_ref, c0_ref, n0_ref, m0_ref,
               out_ref, cout_ref, nout_ref, mout_ref, c_s, n_s, m_s, *, L, hd):
    ci = pl.program_id(1)

    @pl.when(ci == 0)
    def _():
        c_s[...] = c0_ref[0]
        n_s[...] = n0_ref[0]
        m_s[...] = m0_ref[0]

    gates = gate_ref[...] + bias_ref[...]
    glane = lax.broadcasted_iota(jnp.int32, gates.shape, 1)
    row = lax.broadcasted_iota(jnp.int32, (L, L), 0)
    col = lax.broadcasted_iota(jnp.int32, (L, L), 1)
    tri = col <= row
    eye = col == row
    m_all = m_s[...]
    mlane = lax.broadcasted_iota(jnp.int32, m_all.shape, 1)
    m_next = jnp.zeros_like(m_all)

    def lane_pick(x, lanes, idx):
        return jnp.sum(jnp.where(lanes == idx, x, 0.0), axis=-1, keepdims=True)

    def to_row(x_col):
        return jnp.sum(jnp.where(eye, x_col, 0.0), axis=0, keepdims=True)

    for h in range(ML_HEADS):
        sl = slice(h * hd, (h + 1) * hd)
        ic = lane_pick(gates, glane, h)
        lf = jax.nn.log_sigmoid(lane_pick(gates, glane, ML_HEADS + h))
        m_prev = lane_pick(m_all, mlane, h)
        ic_row = to_row(ic)
        b_col = jnp.sum(jnp.where(tri, to_row(lf), 0.0), axis=-1, keepdims=True)
        b_row = to_row(b_col)
        dmat = jnp.where(tri, b_col - b_row + ic_row, -jnp.inf)
        inter = b_col + m_prev
        m_t = jnp.maximum(inter, jnp.max(dmat, axis=-1, keepdims=True))
        w = jnp.exp(dmat - m_t)
        g = jnp.exp(inter - m_t)

        qc = q_ref[:, sl]
        kc = k_ref[:, sl] * (hd ** -0.5)
        vc = v_ref[:, sl]
        c_old = c_s[h]
        n_old = n_s[h]
        s = lax.dot_general(qc, kc, (((1,), (1,)), ((), ())), preferred_element_type=F32) * w
        num = (jnp.dot(s.astype(BF16), vc, preferred_element_type=F32)
               + g * lax.dot_general(qc, c_old.astype(BF16), (((1,), (1,)), ((), ())),
                                     preferred_element_type=F32))
        den = (jnp.sum(s, axis=-1, keepdims=True)
               + g * jnp.sum(qc.astype(F32) * n_old, axis=-1, keepdims=True))
        hm = num / jnp.maximum(jnp.abs(den), jnp.exp(-m_t))

        m_new = m_t[L - 1:L]
        wl = jnp.exp(b_col[L - 1:L] - b_col + ic - m_new)
        gl = jnp.exp(inter[L - 1:L] - m_new)
        kf = kc.astype(F32)
        vw = (vc.astype(F32) * wl).astype(BF16)
        c_s[h] = gl * c_old + lax.dot_general(vw, kc, (((0,), (0,)), ((), ())), preferred_element_type=F32)
        n_s[h] = gl * n_old + jnp.sum(wl * kf, axis=0, keepdims=True)
        m_next = m_next + jnp.where(mlane == h, m_new, 0.0)

        y = _rms(hm, gml_ref[:, sl]) * jax.nn.sigmoid(og_ref[:, sl].astype(F32))
        out_ref[:, sl] = y.astype(out_ref.dtype)

    m_s[...] = m_next

    @pl.when(ci == pl.num_programs(1) - 1)
    def _():
        cout_ref[0] = c_s[...]
        nout_ref[0] = n_s[...]
        mout_ref[0] = m_s[...]


def _mlstm(mq, mk, mv, mo, gates, bias, g_ml, c0, n0, m0, batch, L):
    n, w = mq.shape
    hd = w // ML_HEADS
    nc = n // batch // L
    rows = lambda b, c: (b * nc + c, 0)
    const = lambda b, c: (0, 0)
    st4 = lambda b, c: (b, 0, 0, 0)
    st3 = lambda b, c: (b, 0, 0)
    n0 = n0.reshape(batch, ML_HEADS, 1, hd)
    m0 = jnp.pad(m0, ((0, 0), (0, LANES - ML_HEADS))).reshape(batch, 1, LANES)
    resident = (4 * _nbytes((1, ML_HEADS, hd, hd), F32) + _nbytes((ML_HEADS, hd, hd), F32)
                + 10 * _nbytes((L, w), BF16) + 4 * _nbytes((L, LANES), F32) + 8 * _nbytes((hd, hd), F32))
    out, c, nn, m = pl.pallas_call(
        functools.partial(_ml_kernel, L=L, hd=hd),
        grid=(batch, nc),
        in_specs=[pl.BlockSpec((L, w), rows), pl.BlockSpec((L, w), rows), pl.BlockSpec((L, w), rows),
                  pl.BlockSpec((L, w), rows), pl.BlockSpec((L, LANES), rows),
                  pl.BlockSpec((1, LANES), const), pl.BlockSpec((1, w), const),
                  pl.BlockSpec((1, ML_HEADS, hd, hd), st4), pl.BlockSpec((1, ML_HEADS, 1, hd), st4),
                  pl.BlockSpec((1, 1, LANES), st3)],
        out_specs=[pl.BlockSpec((L, w), rows), pl.BlockSpec((1, ML_HEADS, hd, hd), st4),
                   pl.BlockSpec((1, ML_HEADS, 1, hd), st4), pl.BlockSpec((1, 1, LANES), st3)],
        out_shape=[jax.ShapeDtypeStruct((n, w), BF16), jax.ShapeDtypeStruct((batch, ML_HEADS, hd, hd), F32),
                   jax.ShapeDtypeStruct((batch, ML_HEADS, 1, hd), F32),
                   jax.ShapeDtypeStruct((batch, 1, LANES), F32)],
        scratch_shapes=[pltpu.VMEM((ML_HEADS, hd, hd), F32), pltpu.VMEM((ML_HEADS, 1, hd), F32),
                        pltpu.VMEM((1, LANES), F32)],
        compiler_params=pltpu.CompilerParams(
            dimension_semantics=("arbitrary", "arbitrary"),
            vmem_limit_bytes=_vmem_limit(resident)),
        name="mlstm",
    )(mq, mk, mv, mo, gates, bias, g_ml, c0, n0, m0)
    return out, c, nn.reshape(batch, ML_HEADS, hd), m[:, 0, :ML_HEADS]


def _matmul_res_kernel(a_ref, b_ref, wa_ref, wb_ref, r_ref, o_ref):
    o_ref[...] = (r_ref[...] + jnp.dot(a_ref[...], wa_ref[...], preferred_element_type=F32)
                  + jnp.dot(b_ref[...], wb_ref[...], preferred_element_type=F32))


def _matmul_res(a, a_blk, b, b_blk, w, resid, tm, tn, name):
    n, dout = resid.shape
    kh = w.shape[0] // 2
    resident = 2 * (2 * _nbytes((tm, kh), BF16) + 2 * _nbytes((kh, tn), BF16) + 3 * _nbytes((tm, tn), F32))
    return pl.pallas_call(
        _matmul_res_kernel,
        grid=(n // tm, dout // tn),
        in_specs=[pl.BlockSpec((tm, kh), lambda i, j: (i, a_blk)),
                  pl.BlockSpec((tm, kh), lambda i, j: (i, b_blk)),
                  pl.BlockSpec((kh, tn), lambda i, j: (0, j)),
                  pl.BlockSpec((kh, tn), lambda i, j: (1, j)),
                  pl.BlockSpec((tm, tn), lambda i, j: (i, j))],
        out_specs=pl.BlockSpec((tm, tn), lambda i, j: (i, j)),
        out_shape=jax.ShapeDtypeStruct((n, dout), F32),
        compiler_params=pltpu.CompilerParams(
            dimension_semantics=("arbitrary", "arbitrary"),
            vmem_limit_bytes=_vmem_limit(resident)),
        name=name,
    )(a, b, w, w, resid)


def _mem_attn_kernel(q_ref, k_ref, v_ref, o_ref, *, hd):
    for h in range(MEM_HEADS):
        sl = slice(h * hd, (h + 1) * hd)
        s = lax.dot_general(q_ref[:, sl], k_ref[0, :, sl], (((1,), (1,)), ((), ())),
                            preferred_element_type=F32) * (hd ** -0.5)
        p = jnp.exp(s - jnp.max(s, axis=-1, keepdims=True))
        a = p / jnp.sum(p, axis=-1, keepdims=True)
        o_ref[:, sl] = jnp.dot(a.astype(BF16), v_ref[0, :, sl], preferred_element_type=F32).astype(o_ref.dtype)


def _mem_attn(q, mem_k, mem_v, batch, tq):
    n, d = q.shape
    mlen = mem_k.shape[1]
    nq = n // batch // tq
    resident = 4 * _nbytes((tq, d), BF16) + 4 * _nbytes((mlen, d), BF16) + 6 * _nbytes((tq, mlen), F32)
    return pl.pallas_call(
        functools.partial(_mem_attn_kernel, hd=d // MEM_HEADS),
        grid=(batch, nq),
        in_specs=[pl.BlockSpec((tq, d), lambda b, i: (b * nq + i, 0)),
                  pl.BlockSpec((1, mlen, d), lambda b, i: (b, 0, 0)),
                  pl.BlockSpec((1, mlen, d), lambda b, i: (b, 0, 0))],
        out_specs=pl.BlockSpec((tq, d), lambda b, i: (b * nq + i, 0)),
        out_shape=jax.ShapeDtypeStruct((n, d), BF16),
        compiler_params=pltpu.CompilerParams(
            dimension_semantics=("arbitrary", "arbitrary"),
            vmem_limit_bytes=_vmem_limit(resident)),
        name="mem_attn",
    )(q, mem_k, mem_v)


def _ffn_kernel(x_ref, g_ref, wg_ref, wu_ref, wd_ref, cw_ref, cb_ref, past_ref, gf_ref,
                y_ref, conv_ref, hn_ref, acc_ref, carry_ref, *, tm, blocks_per_batch):
    i, j = pl.program_id(0), pl.program_id(1)

    @pl.when((i == 0) & (j == 0))
    def _():
        carry_ref[...] = jnp.zeros(carry_ref.shape, F32)

    @pl.when(j == 0)
    def _():
        hn_ref[...] = _rms(x_ref[...], g_ref[...]).astype(BF16)
        acc_ref[...] = jnp.zeros(acc_ref.shape, F32)

    hn = hn_ref[...]
    g = jnp.dot(hn, wg_ref[...], preferred_element_type=F32)
    u = jnp.dot(hn, wu_ref[...], preferred_element_type=F32)

    first = (i % blocks_per_batch) == 0
    prev = jnp.where(first, past_ref[0], carry_ref[j])
    tail = g[tm - (CONV_W - 1):tm]
    carry_ref[j] = tail
    conv_ref[0] = tail

    row = lax.broadcasted_iota(jnp.int32, g.shape, 0)
    g1 = jnp.where(row == 0, prev[1:2], pltpu.roll(g, 1, 0))
    g2 = jnp.where(row == 0, prev[0:1], jnp.where(row == 1, prev[1:2], pltpu.roll(g, 2, 0)))
    cw = cw_ref[...]
    c = cb_ref[...] + cw[0:1] * g2 + cw[1:2] * g1 + cw[2:3] * g
    act = (c * jax.nn.sigmoid(c)) * u
    acc_ref[...] += jnp.dot(act.astype(BF16), wd_ref[...], preferred_element_type=F32)

    @pl.when(j == pl.num_programs(1) - 1)
    def _():
        y_ref[...] = _rms(x_ref[...] + acc_ref[...], gf_ref[...])


def _ffn(x, g_ffn, wg, wu, wd, conv_w, conv_b, past, g_final, batch, tm, tf):
    n, d = x.shape
    dff = wg.shape[1]
    nj = dff // tf
    bpb = n // batch // tm
    resident = (4 * _nbytes((tm, d), F32) + _nbytes((tm, d), BF16) + _nbytes((tm, d), F32)
                + 6 * _nbytes((d, tf), BF16) + 8 * _nbytes((tm, tf), F32) + 2 * _nbytes((2, dff), F32))
    y, conv = pl.pallas_call(
        functools.partial(_ffn_kernel, tm=tm, blocks_per_batch=bpb),
        grid=(n // tm, nj),
        in_specs=[pl.BlockSpec((tm, d), lambda i, j: (i, 0)),
                  pl.BlockSpec((1, d), lambda i, j: (0, 0)),
                  pl.BlockSpec((d, tf), lambda i, j: (0, j)),
                  pl.BlockSpec((d, tf), lambda i, j: (0, j)),
                  pl.BlockSpec((tf, d), lambda i, j: (j, 0)),
                  pl.BlockSpec((CONV_W, tf), lambda i, j: (0, j)),
                  pl.BlockSpec((1, tf), lambda i, j: (0, j)),
                  pl.BlockSpec((1, CONV_W - 1, tf), lambda i, j: (i // bpb, 0, j)),
                  pl.BlockSpec((1, d), lambda i, j: (0, 0))],
        out_specs=[pl.BlockSpec((tm, d), lambda i, j: (i, 0)),
                   pl.BlockSpec((1, CONV_W - 1, tf), lambda i, j: (i, 0, j))],
        out_shape=[jax.ShapeDtypeStruct((n, d), F32),
                   jax.ShapeDtypeStruct((n // tm, CONV_W - 1, dff), F32)],
        scratch_shapes=[pltpu.VMEM((tm, d), BF16), pltpu.VMEM((tm, d), F32),
                        pltpu.VMEM((nj, CONV_W - 1, tf), F32)],
        compiler_params=pltpu.CompilerParams(
            dimension_semantics=("arbitrary", "arbitrary"),
            vmem_limit_bytes=_vmem_limit(resident)),
        name="conv_ffn",
    )(x, g_ffn, wg, wu, wd, conv_w, conv_b, past, g_final)
    return y, conv[bpb - 1::bpb]


def _pad_cols(a, width):
    return jnp.pad(a, ((0, 0), (0, width - a.shape[1])))


def _layer_pass(x, wts, batch, seq, da_cache, ml_state, conv_past, mem_kv, lam_init, ml_chunk, tiles):
    n, d = x.shape
    tm = tiles["tm"]
    width = wts["da_width"]
    plan = [(0, BF16), (1, F32), (1, BF16), (2, F32), (2, BF16), (3, BF16), (4, BF16), (5, BF16), (6, BF16)]
    (dq, k_f32, dk, v_f32, dv, mq, mk, mv, mo, gates) = _norm_proj(
        x, wts["g_mix"], wts["w_in"], 7, width, plan, tm, tiles["tn_in"],
        gate_w=(wts["w_gate_hi"], wts["w_gate_lo"]), name="in_proj")

    if da_cache is None:
        da_out = _da_prompt(dq, dk, dv, wts["lams"], wts["g_da"], batch, seq, lam_init, tiles["tq"])
    else:
        da_out = _da_sample(dq, dk, dv, da_cache[0], da_cache[1], wts["lams"], wts["g_da"], lam_init,
                            tiles["tk_cache"])

    ml_out, c_new, n_new, m_new = _mlstm(mq, mk, mv, mo, gates, wts["gate_bias"], wts["g_ml"],
                                         ml_state[0], ml_state[1], ml_state[2], batch, ml_chunk)

    x1 = _matmul_res(da_out, 0, ml_out, 0, wts["w_out"], x, tm, tiles["tn"], "out_proj")

    (mq_x,) = _norm_proj(x1, wts["g_xattn"], wts["w_mq"], 1, d, [(0, BF16)], tm, tiles["tn"], name="memq_proj")
    mo_x = _mem_attn(mq_x, mem_kv[0], mem_kv[1], batch, min(tm, seq))
    x2 = _matmul_res(mo_x, 0, mo_x, 1, wts["w_mo"], x1, tm, tiles["tn"], "memo_proj")

    y, conv_new = _ffn(x2, wts["g_ffn"], wts["w_ffn_gate"], wts["w_ffn_up"], wts["w_ffn_down"],
                       wts["conv_w"], wts["conv_b"], conv_past, wts["g_final"], batch,
                       tiles["tm_ffn"], tiles["tf"])
    return y, k_f32, v_f32, (c_new, n_new, m_new), conv_new


def kernel(x_prompt, x_sample, cache_da_k, cache_da_v, state_ml_c, state_ml_n, state_ml_m, state_ffn_conv,
           cache_mem_k, cache_mem_v, mem_prompt, w_in, g_mix, lambda_q1, lambda_k1, lambda_q2, lambda_k2,
           g_da_sub, b_ig, b_fg, g_ml, w_out, g_xattn, g_mem, w_mq, w_mk, w_mv, w_mo, g_ffn, w_gate, w_up,
           conv_w, conv_b, w_down, g_final):
    depth = w_in.shape[0]
    assert depth == 1, "single-layer encoder"
    batch, seq, d = x_prompt.shape
    dbatch, dseq, _ = x_sample.shape
    past = cache_da_k.shape[2]
    mlen = mem_prompt.shape[1]
    dff = w_gate.shape[2]
    da_width = d // 2
    ml_width = d - da_width
    hd = ml_width // ML_HEADS
    assert seq % 512 == 0 and past % 1024 == 0 and past % CHUNK == 0 and dseq <= CHUNK
    tf = 512
    dff_p = -(-dff // tf) * tf
    l = 0
    lam_init = 0.8 - 0.6 * math.exp(-0.3 * l)

    n_main = 3 * da_width + 4 * ml_width
    w_in_l = w_in[l]
    w_g = _pad_cols(w_in_l[:, n_main:], LANES)
    w_g_hi = w_g.astype(BF16)
    wts = dict(
        da_width=da_width,
        w_in=w_in_l[:, :n_main].astype(BF16),
        w_gate_hi=w_g_hi,
        w_gate_lo=(w_g - w_g_hi.astype(F32)).astype(BF16),
        gate_bias=_pad_cols(jnp.concatenate([b_ig[l], b_fg[l]])[None, :], LANES),
        g_mix=g_mix[l], g_xattn=g_xattn[l], g_ffn=g_ffn[l].reshape(1, d), g_final=g_final.reshape(1, d),
        lams=jnp.stack([lambda_q1[l], lambda_k1[l], lambda_q2[l], lambda_k2[l]]),
        g_da=g_da_sub[l].reshape(1, 2 * DA_HEAD_DIM),
        g_ml=g_ml[l].reshape(1, ml_width),
        w_out=w_out[l].astype(BF16), w_mq=w_mq[l].astype(BF16), w_mo=w_mo[l].astype(BF16),
        w_ffn_gate=_pad_cols(w_gate[l], dff_p).astype(BF16),
        w_ffn_up=_pad_cols(w_up[l], dff_p).astype(BF16),
        w_ffn_down=jnp.pad(w_down[l], ((0, dff_p - dff), (0, 0))).astype(BF16),
        conv_w=_pad_cols(conv_w[l], dff_p),
        conv_b=_pad_cols(conv_b[l][None, :], dff_p),
    )

    w_mkv = jnp.concatenate([w_mk[l], w_mv[l]], axis=1).astype(BF16)
    mk_f32, mk_bf, mv_f32, mv_bf = _norm_proj(
        mem_prompt.reshape(batch * mlen, d), g_mem[l], w_mkv, 2, d,
        [(0, F32), (0, BF16), (1, F32), (1, BF16)], 512, 512, name="memkv_proj")
    zero_state = (jnp.zeros((batch, ML_HEADS, hd, hd), F32), jnp.zeros((batch, ML_HEADS, hd), F32),
                  jnp.zeros((batch, ML_HEADS), F32))
    tiles_p = dict(tm=512, tn_in=256, tn=512, tq=512, tm_ffn=512, tf=tf)
    yp, pk, pv, (pc, pn, pm), pconv = _layer_pass(
        x_prompt.reshape(batch * seq, d), wts, batch, seq, None, zero_state,
        jnp.zeros((batch, CONV_W - 1, dff_p), F32),
        (mk_bf.reshape(batch, mlen, d), mv_bf.reshape(batch, mlen, d)), lam_init, CHUNK, tiles_p)

    ns = dbatch * dseq
    tiles_s = dict(tm=ns, tn_in=256, tn=512, tk_cache=1024, tm_ffn=dseq, tf=tf)
    ys, sk, sv, (sc, sn, sm), sconv = _layer_pass(
        x_sample.reshape(ns, d), wts, dbatch, dseq,
        (cache_da_k[l].reshape(dbatch, past, da_width), cache_da_v[l].reshape(dbatch, past, da_width)),
        (state_ml_c[l], state_ml_n[l], state_ml_m[l]),
        _pad_cols(state_ffn_conv[l].reshape(dbatch * (CONV_W - 1), dff), dff_p).reshape(dbatch, CONV_W - 1, dff_p),
        (cache_mem_k[l].reshape(dbatch, mlen, d).astype(BF16), cache_mem_v[l].reshape(dbatch, mlen, d).astype(BF16)),
        lam_init, dseq, tiles_s)

    kv_shape_p = (1, batch, seq, DA_HEADS, 2 * DA_HEAD_DIM)
    kv_shape_s = (1, dbatch, dseq, DA_HEADS, 2 * DA_HEAD_DIM)
    mem_shape = (1, batch, mlen, MEM_HEADS, d // MEM_HEADS)
    return (yp.reshape(batch, seq, d), ys.reshape(dbatch, dseq, d),
            pk.reshape(kv_shape_p), pv.reshape(kv_shape_p), pc[None], pn[None], pm[None],
            pconv[None, :, :, :dff], mk_f32.reshape(mem_shape), mv_f32.reshape(mem_shape),
            sk.reshape(kv_shape_s), sv.reshape(kv_shape_s), sc[None], sn[None], sm[None],
            sconv[None, :, :, :dff])
```

```python
import functools
import math

import jax
import jax.numpy as jnp
from jax import lax
from jax.experimental import pallas as pl
from jax.experimental.pallas import tpu as pltpu

F32 = jnp.float32
BF16 = jnp.bfloat16

EPS = 1e-6
NEG = -1e30
CHUNK = 64
DA_HEADS = 8
DA_HEAD_DIM = 64
ML_HEADS = 4
MEM_HEADS = 4
CONV_W = 3
DA_QSCALE = (DA_HEAD_DIM ** -0.5) * math.log2(math.e)

LANES = 128
V7X_VMEM_BYTES = 64 * 1024 * 1024
VMEM_CEILING = V7X_VMEM_BYTES - 8 * 1024 * 1024


def _vmem_limit(block_bytes):
    return int(min(VMEM_CEILING, block_bytes * 1.25 + 8 * 1024 * 1024))


def _nbytes(shape, dtype):
    return math.prod(shape) * jnp.dtype(dtype).itemsize


def _rms(x, g):
    return x * lax.rsqrt(jnp.mean(x * x, axis=-1, keepdims=True) + EPS) * g


def _proj_kernel(*refs, n_pieces, out_plan, has_gate):
    x_ref, g_ref = refs[0], refs[1]
    w_refs = refs[2:2 + n_pieces]
    pos = 2 + n_pieces
    if has_gate:
        wgh_ref, wgl_ref = refs[pos], refs[pos + 1]
        pos += 2
    out_refs = refs[pos:pos + len(out_plan)]
    pos += len(out_plan)
    if has_gate:
        gate_ref = refs[pos]
        pos += 1
    xn_ref = refs[pos]

    @pl.when(pl.program_id(1) == 0)
    def _():
        xn = _rms(x_ref[...], g_ref[...])
        hi = xn.astype(BF16)
        xn_ref[...] = hi
        if has_gate:
            lo = (xn - hi.astype(F32)).astype(BF16)
            gate_ref[...] = (jnp.dot(hi, wgh_ref[...], preferred_element_type=F32)
                             + jnp.dot(lo, wgh_ref[...], preferred_element_type=F32)
                             + jnp.dot(hi, wgl_ref[...], preferred_element_type=F32))

    xn = xn_ref[...]
    for p in range(n_pieces):
        acc = jnp.dot(xn, w_refs[p][...], preferred_element_type=F32)
        for (piece, _, scale), o_ref in zip(out_plan, out_refs):
            if piece == p:
                o_ref[...] = (acc if scale is None else acc * scale).astype(o_ref.dtype)


def _norm_proj(x, g, w, n_pieces, width, out_plan, tm, tn, gate_w=None, name="proj"):
    n, d = x.shape
    nj = width // tn
    in_specs = [pl.BlockSpec((tm, d), lambda i, j: (i, 0)),
                pl.BlockSpec((1, d), lambda i, j: (0, 0))]
    args = [x, g.reshape(1, d)]
    for p in range(n_pieces):
        in_specs.append(pl.BlockSpec((d, tn), functools.partial(lambda i, j, p: (0, p * nj + j), p=p)))
        args.append(w)
    has_gate = gate_w is not None
    if has_gate:
        for gw in gate_w:
            in_specs.append(pl.BlockSpec((d, LANES), lambda i, j: (0, 0)))
            args.append(gw)
    out_shape = [jax.ShapeDtypeStruct((n, width), dt) for _, dt, _ in out_plan]
    out_specs = [pl.BlockSpec((tm, tn), lambda i, j: (i, j)) for _ in out_plan]
    if has_gate:
        out_shape.append(jax.ShapeDtypeStruct((n, LANES), F32))
        out_specs.append(pl.BlockSpec((tm, LANES), lambda i, j: (i, 0)))
    resident = (2 * _nbytes((tm, d), F32) + _nbytes((tm, d), BF16)
                + 2 * n_pieces * _nbytes((d, tn), BF16)
                + 2 * sum(_nbytes((tm, tn), dt) for _, dt, _ in out_plan)
                + (4 * _nbytes((d, LANES), BF16) + 2 * _nbytes((tm, LANES), F32) if has_gate else 0)
                + 2 * _nbytes((tm, tn), F32))
    return pl.pallas_call(
        functools.partial(_proj_kernel, n_pieces=n_pieces, out_plan=tuple(out_plan), has_gate=has_gate),
        grid=(n // tm, nj),
        in_specs=in_specs,
        out_specs=out_specs,
        out_shape=out_shape,
        scratch_shapes=[pltpu.VMEM((tm, d), BF16)],
        compiler_params=pltpu.CompilerParams(
            dimension_semantics=("arbitrary", "arbitrary"),
            vmem_limit_bytes=_vmem_limit(resident)),
        name=name,
    )(*args)


def _da_update(q, k, v, mask, m_ref, l_ref, acc_ref, h):
    lane = lax.broadcasted_iota(jnp.int32, k.shape, 1)
    zero = jnp.zeros_like(k)
    for c in range(2):
        kc = jnp.where((lane < DA_HEAD_DIM) if c == 0 else (lane >= DA_HEAD_DIM), k, zero)
        s = lax.dot_general(q, kc, (((1,), (1,)), ((), ())), preferred_element_type=F32)
        if mask is not None:
            s = jnp.where(mask, s, NEG)
        idx = 2 * h + c
        m_prev = m_ref[idx]
        m_new = jnp.maximum(m_prev, jnp.max(s, axis=-1, keepdims=True))
        p = jnp.exp2(s - m_new)
        alpha = jnp.exp2(m_prev - m_new)
        l_ref[idx] = alpha * l_ref[idx] + jnp.sum(p, axis=-1, keepdims=True)
        m_ref[idx] = m_new
        pv = jnp.dot(p.astype(BF16), v, preferred_element_type=F32)
        sl = slice(idx * LANES, (idx + 1) * LANES)
        acc_ref[:, sl] = alpha * acc_ref[:, sl] + pv


def _da_init(m_ref, l_ref, acc_ref):
    m_ref[...] = jnp.full(m_ref.shape, NEG, F32)
    l_ref[...] = jnp.zeros(l_ref.shape, F32)
    acc_ref[...] = jnp.zeros(acc_ref.shape, F32)


def _da_finalize(lams_ref, gda_ref, m_ref, l_ref, acc_ref, o_ref, lam_init):
    lam = _da_lambda(lams_ref, lam_init)
    gda = gda_ref[...]
    for h in range(DA_HEADS):
        o0 = acc_ref[:, (2 * h) * LANES:(2 * h + 1) * LANES] / l_ref[2 * h]
        o1 = acc_ref[:, (2 * h + 1) * LANES:(2 * h + 2) * LANES] / l_ref[2 * h + 1]
        o = o0 - lam * o1
        o_ref[:, h * LANES:(h + 1) * LANES] = (_rms(o, gda) * (1.0 - lam_init)).astype(o_ref.dtype)


def _chunk_mask(q0, k0, tq, tk):
    q_pos = q0 + lax.broadcasted_iota(jnp.int32, (tq, tk), 0)
    k_pos = k0 + lax.broadcasted_iota(jnp.int32, (tq, tk), 1)
    return (k_pos // CHUNK) <= (q_pos // CHUNK)


def _da_lambda(lams_ref, lam_init):
    lams = lams_ref[...]
    return (jnp.exp(jnp.sum(lams[0:1] * lams[1:2], axis=-1, keepdims=True))
            - jnp.exp(jnp.sum(lams[2:3] * lams[3:4], axis=-1, keepdims=True)) + lam_init)


def _da_prompt_kernel(lams_ref, gda_ref, q_ref, k_ref, v_ref, o_ref, qc_ref, m_ref, l_ref, acc_ref, *, tq, lam_init):
    qi, ki = pl.program_id(1), pl.program_id(2)

    @pl.when(ki == 0)
    def _():
        m_ref[...] = jnp.full(m_ref.shape, NEG, F32)
        l_ref[...] = jnp.zeros(l_ref.shape, F32)
        acc_ref[...] = jnp.zeros(acc_ref.shape, F32)
        qs = q_ref[...]
        lane = lax.broadcasted_iota(jnp.int32, qs.shape, 1) % LANES
        zero = jnp.zeros_like(qs)
        qc_ref[0] = jnp.where(lane < DA_HEAD_DIM, qs, zero)
        qc_ref[1] = jnp.where(lane >= DA_HEAD_DIM, qs, zero)

    def scores(idx):
        h, c = divmod(idx, 2)
        sl = slice(h * LANES, (h + 1) * LANES)
        return lax.dot_general(k_ref[:, sl], qc_ref[c, :, sl], (((1,), (1,)), ((), ())),
                               preferred_element_type=F32)

    def key_block(mask):
        st_next = scores(0)
        for idx in range(2 * DA_HEADS):
            st = st_next
            if idx + 1 < 2 * DA_HEADS:
                st_next = scores(idx + 1)
            if mask is not None:
                st = jnp.where(mask, st, NEG)
            m_prev = m_ref[idx]
            m_new = jnp.maximum(m_prev, jnp.max(st, axis=0, keepdims=True))
            p = jnp.exp2(st - m_new)
            alpha = jnp.exp2(m_prev - m_new)
            l_ref[idx] = alpha * l_ref[idx] + jnp.sum(p, axis=0, keepdims=True)
            m_ref[idx] = m_new
            sl = slice((idx // 2) * LANES, (idx // 2 + 1) * LANES)
            pv = lax.dot_general(v_ref[:, sl], p.astype(BF16), (((0,), (0,)), ((), ())),
                                 preferred_element_type=F32)
            acc_ref[idx] = alpha * acc_ref[idx] + pv

    @pl.when(ki < qi)
    def _():
        key_block(None)

    @pl.when(ki == qi)
    def _():
        k_pos = lax.broadcasted_iota(jnp.int32, (tq, tq), 0)
        q_pos = lax.broadcasted_iota(jnp.int32, (tq, tq), 1)
        key_block((k_pos // CHUNK) <= (q_pos // CHUNK))
        lam = _da_lambda(lams_ref, lam_init)
        gda = gda_ref[...]
        for h in range(DA_HEADS):
            o0 = acc_ref[2 * h] * (1.0 / l_ref[2 * h])
            o1 = acc_ref[2 * h + 1] * (1.0 / l_ref[2 * h + 1])
            o = o0 - lam * o1
            y = o * lax.rsqrt(jnp.mean(o * o, axis=0, keepdims=True) + EPS) * gda * (1.0 - lam_init)
            o_ref[:, h * LANES:(h + 1) * LANES] = y.T.astype(o_ref.dtype)


def _da_prompt(q, k, v, lams, g_da, batch, seq, lam_init, tq):
    n, w = q.shape
    nq = seq // tq
    qmap = lambda b, qi, ki: (b * nq + qi, 0)
    kmap = lambda b, qi, ki: (b * nq + jnp.minimum(ki, qi), 0)
    const = lambda b, qi, ki: (0, 0)
    g_col = g_da.reshape(LANES, 1)
    resident = (8 * _nbytes((tq, w), BF16) + 2 * _nbytes((tq, w), BF16) + _nbytes((tq, 2 * w), F32)
                + 8 * _nbytes((tq, tq), F32))
    return pl.pallas_call(
        functools.partial(_da_prompt_kernel, tq=tq, lam_init=lam_init),
        grid=(batch, nq, nq),
        in_specs=[pl.BlockSpec(lams.shape, const), pl.BlockSpec(g_col.shape, const),
                  pl.BlockSpec((tq, w), qmap), pl.BlockSpec((tq, w), kmap), pl.BlockSpec((tq, w), kmap)],
        out_specs=pl.BlockSpec((tq, w), qmap),
        out_shape=jax.ShapeDtypeStruct((n, w), BF16),
        scratch_shapes=[pltpu.VMEM((2, tq, w), BF16),
                        pltpu.VMEM((2 * DA_HEADS, 1, tq), F32), pltpu.VMEM((2 * DA_HEADS, 1, tq), F32),
                        pltpu.VMEM((2 * DA_HEADS, LANES, tq), F32)],
        compiler_params=pltpu.CompilerParams(
            dimension_semantics=("arbitrary", "arbitrary", "arbitrary"),
            vmem_limit_bytes=_vmem_limit(resident)),
        name="da_prompt",
    )(lams, g_col, q, k, v)


def _da_sample_kernel(lams_ref, gda_ref, q_ref, kn_ref, vn_ref, ck_ref, cv_ref, o_ref, m_ref, l_ref, acc_ref,
                      *, t, tk, past, lam_init):
    ki = pl.program_id(1)

    @pl.when(ki == 0)
    def _():
        _da_init(m_ref, l_ref, acc_ref)

    kblk = ck_ref[0].astype(BF16)
    vblk = cv_ref[0].astype(BF16)
    mask = _chunk_mask(past, ki * tk, t, tk)
    for h in range(DA_HEADS):
        sl = slice(h * LANES, (h + 1) * LANES)
        _da_update(q_ref[:, sl], kblk[:, sl], vblk[:, sl], mask, m_ref, l_ref, acc_ref, h)

    @pl.when(ki == pl.num_programs(1) - 1)
    def _():
        mask_new = _chunk_mask(past, past, t, t)
        for h in range(DA_HEADS):
            sl = slice(h * LANES, (h + 1) * LANES)
            _da_update(q_ref[:, sl], kn_ref[:, sl], vn_ref[:, sl], mask_new, m_ref, l_ref, acc_ref, h)
        _da_finalize(lams_ref, gda_ref, m_ref, l_ref, acc_ref, o_ref, lam_init)


def _da_sample(q, kn, vn, cache_k, cache_v, lams, g_da, lam_init, tk):
    n, w = q.shape
    batch, past, _ = cache_k.shape
    t = n // batch
    const = lambda b, ki: (0, 0)
    rows = lambda b, ki: (b, 0)
    cmap = lambda b, ki: (b, ki, 0)
    resident = (4 * _nbytes((tk, w), F32) + 2 * _nbytes((tk, w), BF16) + 8 * _nbytes((t, w), BF16)
                + _nbytes((t, 2 * w), F32) + 4 * DA_HEADS * _nbytes((t, LANES), F32))
    return pl.pallas_call(
        functools.partial(_da_sample_kernel, t=t, tk=tk, past=past, lam_init=lam_init),
        grid=(batch, past // tk),
        in_specs=[pl.BlockSpec(lams.shape, const), pl.BlockSpec(g_da.shape, const),
                  pl.BlockSpec((t, w), rows), pl.BlockSpec((t, w), rows), pl.BlockSpec((t, w), rows),
                  pl.BlockSpec((1, tk, w), cmap), pl.BlockSpec((1, tk, w), cmap)],
        out_specs=pl.BlockSpec((t, w), rows),
        out_shape=jax.ShapeDtypeStruct((n, w), BF16),
        scratch_shapes=[pltpu.VMEM((2 * DA_HEADS, t, 1), F32), pltpu.VMEM((2 * DA_HEADS, t, 1), F32),
                        pltpu.VMEM((t, 2 * w), F32)],
        compiler_params=pltpu.CompilerParams(
            dimension_semantics=("arbitrary", "arbitrary"),
            vmem_limit_bytes=_vmem_limit(resident)),
        name="da_sample",
    )(lams, g_da, q, kn, vn, cache_k, cache_v)


def _ml_kernel(q_ref, k_ref, v_ref, og_ref, gate_ref, bias_ref, gml_ref, c0_ref, n0_ref, m0_ref,
               out_ref, cout_ref, nout_ref, mout_ref, c_s, n_s, m_s, *, L, hd):
    ci = pl.program_id(1)

    @pl.when(ci == 0)
    def _():
        c_s[...] = c0_ref[0]
        n_s[...] = n0_ref[0]
        m_s[...] = m0_ref[0]

    gates = gate_ref[...] + bias_ref[...]
    glane = lax.broadcasted_iota(jnp.int32, gates.shape, 1)
    row = lax.broadcasted_iota(jnp.int32, (L, L), 0)
    col = lax.broadcasted_iota(jnp.int32, (L, L), 1)
    tri = col <= row
    eye = col == row
    m_all = m_s[...]
    mlane = lax.broadcasted_iota(jnp.int32, m_all.shape, 1)
    m_next = jnp.zeros_like(m_all)

    def lane_pick(x, lanes, idx):
        return jnp.sum(jnp.where(lanes == idx, x, 0.0), axis=-1, keepdims=True)

    def to_row(x_col):
        return jnp.sum(jnp.where(eye, x_col, 0.0), axis=0, keepdims=True)

    for h in range(ML_HEADS):
        sl = slice(h * hd, (h + 1) * hd)
        ic = lane_pick(gates, glane, h)
        lf = jax.nn.log_sigmoid(lane_pick(gates, glane, ML_HEADS + h))
        m_prev = lane_pick(m_all, mlane, h)
        ic_row = to_row(ic)
        b_col = jnp.sum(jnp.where(tri, to_row(lf), 0.0), axis=-1, keepdims=True)
        b_row = to_row(b_col)
        dmat = jnp.where(tri, b_col - b_row + ic_row, -jnp.inf)
        inter = b_col + m_prev
        m_t = jnp.maximum(inter, jnp.max(dmat, axis=-1, keepdims=True))
        w = jnp.exp(dmat - m_t)
        g = jnp.exp(inter - m_t)

        qc = q_ref[:, sl]
        kc = k_ref[:, sl] * (hd ** -0.5)
        vc = v_ref[:, sl]
        c_old = c_s[h]
        n_old = n_s[h]
        s = lax.dot_general(qc, kc, (((1,), (1,)), ((), ())), preferred_element_type=F32) * w
        num = (jnp.dot(s.astype(BF16), vc, preferred_element_type=F32)
               + g * lax.dot_general(qc, c_old.astype(BF16), (((1,), (1,)), ((), ())),
                                     preferred_element_type=F32))
        den = (jnp.sum(s, axis=-1, keepdims=True)
               + g * jnp.sum(qc.astype(F32) * n_old, axis=-1, keepdims=True))
        hm = num / jnp.maximum(jnp.abs(den), jnp.exp(-m_t))

        m_new = m_t[L - 1:L]
        wl = jnp.exp(b_col[L - 1:L] - b_col + ic - m_new)
        gl = jnp.exp(inter[L - 1:L] - m_new)
        kf = kc.astype(F32)
        vw = (vc.astype(F32) * wl).astype(BF16)
        c_s[h] = gl * c_old + lax.dot_general(vw, kc, (((0,), (0,)), ((), ())), preferred_element_type=F32)
        n_s[h] = gl * n_old + jnp.sum(wl * kf, axis=0, keepdims=True)
        m_next = m_next + jnp.where(mlane == h, m_new, 0.0)

        y = _rms(hm, gml_ref[:, sl]) * jax.nn.sigmoid(og_ref[:, sl].astype(F32))
        out_ref[:, sl] = y.astype(out_ref.dtype)

    m_s[...] = m_next

    @pl.when(ci == pl.num_programs(1) - 1)
    def _():
        cout_ref[0] = c_s[...]
        nout_ref[0] = n_s[...]
        mout_ref[0] = m_s[...]


def _mlstm(mq, mk, mv, mo, gates, bias, g_ml, c0, n0, m0, batch, L):
    n, w = mq.shape
    hd = w // ML_HEADS
    nc = n // batch // L
    rows = lambda b, c: (b * nc + c, 0)
    const = lambda b, c: (0, 0)
    st4 = lambda b, c: (b, 0, 0, 0)
    st3 = lambda b, c: (b, 0, 0)
    n0 = n0.reshape(batch, ML_HEADS, 1, hd)
    m0 = jnp.pad(m0, ((0, 0), (0, LANES - ML_HEADS))).reshape(batch, 1, LANES)
    resident = (4 * _nbytes((1, ML_HEADS, hd, hd), F32) + _nbytes((ML_HEADS, hd, hd), F32)
                + 10 * _nbytes((L, w), BF16) + 4 * _nbytes((L, LANES), F32) + 8 * _nbytes((hd, hd), F32))
    out, c, nn, m = pl.pallas_call(
        functools.partial(_ml_kernel, L=L, hd=hd),
        grid=(batch, nc),
        in_specs=[pl.BlockSpec((L, w), rows), pl.BlockSpec((L, w), rows), pl.BlockSpec((L, w), rows),
                  pl.BlockSpec((L, w), rows), pl.BlockSpec((L, LANES), rows),
                  pl.BlockSpec((1, LANES), const), pl.BlockSpec((1, w), const),
                  pl.BlockSpec((1, ML_HEADS, hd, hd), st4), pl.BlockSpec((1, ML_HEADS, 1, hd), st4),
                  pl.BlockSpec((1, 1, LANES), st3)],
        out_specs=[pl.BlockSpec((L, w), rows), pl.BlockSpec((1, ML_HEADS, hd, hd), st4),
                   pl.BlockSpec((1, ML_HEADS, 1, hd), st4), pl.BlockSpec((1, 1, LANES), st3)],
        out_shape=[jax.ShapeDtypeStruct((n, w), BF16), jax.ShapeDtypeStruct((batch, ML_HEADS, hd, hd), F32),
                   jax.ShapeDtypeStruct((batch, ML_HEADS, 1, hd), F32),
                   jax.ShapeDtypeStruct((batch, 1, LANES), F32)],
        scratch_shapes=[pltpu.VMEM((ML_HEADS, hd, hd), F32), pltpu.VMEM((ML_HEADS, 1, hd), F32),
                        pltpu.VMEM((1, LANES), F32)],
        compiler_params=pltpu.CompilerParams(
            dimension_semantics=("arbitrary", "arbitrary"),
            vmem_limit_bytes=_vmem_limit(resident)),
        name="mlstm",
    )(mq, mk, mv, mo, gates, bias, g_ml, c0, n0, m0)
    return out, c, nn.reshape(batch, ML_HEADS, hd), m[:, 0, :ML_HEADS]


def _matmul_res_kernel(a_ref, b_ref, wa_ref, wb_ref, r_ref, o_ref):
    o_ref[...] = (r_ref[...] + jnp.dot(a_ref[...], wa_ref[...], preferred_element_type=F32)
                  + jnp.dot(b_ref[...], wb_ref[...], preferred_element_type=F32))


def _matmul_res(a, a_blk, b, b_blk, w, resid, tm, tn, name):
    n, dout = resid.shape
    kh = w.shape[0] // 2
    resident = 2 * (2 * _nbytes((tm, kh), BF16) + 2 * _nbytes((kh, tn), BF16) + 3 * _nbytes((tm, tn), F32))
    return pl.pallas_call(
        _matmul_res_kernel,
        grid=(n // tm, dout // tn),
        in_specs=[pl.BlockSpec((tm, kh), lambda i, j: (i, a_blk)),
                  pl.BlockSpec((tm, kh), lambda i, j: (i, b_blk)),
                  pl.BlockSpec((kh, tn), lambda i, j: (0, j)),
                  pl.BlockSpec((kh, tn), lambda i, j: (1, j)),
                  pl.BlockSpec((tm, tn), lambda i, j: (i, j))],
        out_specs=pl.BlockSpec((tm, tn), lambda i, j: (i, j)),
        out_shape=jax.ShapeDtypeStruct((n, dout), F32),
        compiler_params=pltpu.CompilerParams(
            dimension_semantics=("arbitrary", "arbitrary"),
            vmem_limit_bytes=_vmem_limit(resident)),
        name=name,
    )(a, b, w, w, resid)


def _mem_attn_kernel(q_ref, k_ref, v_ref, o_ref, *, hd):
    for h in range(MEM_HEADS):
        sl = slice(h * hd, (h + 1) * hd)
        s = lax.dot_general(q_ref[:, sl], k_ref[0, :, sl], (((1,), (1,)), ((), ())),
                            preferred_element_type=F32) * (hd ** -0.5)
        p = jnp.exp(s - jnp.max(s, axis=-1, keepdims=True))
        a = p / jnp.sum(p, axis=-1, keepdims=True)
        o_ref[:, sl] = jnp.dot(a.astype(BF16), v_ref[0, :, sl], preferred_element_type=F32).astype(o_ref.dtype)


def _mem_attn(q, mem_k, mem_v, batch, tq):
    n, d = q.shape
    mlen = mem_k.shape[1]
    nq = n // batch // tq
    resident = 4 * _nbytes((tq, d), BF16) + 4 * _nbytes((mlen, d), BF16) + 6 * _nbytes((tq, mlen), F32)
    return pl.pallas_call(
        functools.partial(_mem_attn_kernel, hd=d // MEM_HEADS),
        grid=(batch, nq),
        in_specs=[pl.BlockSpec((tq, d), lambda b, i: (b * nq + i, 0)),
                  pl.BlockSpec((1, mlen, d), lambda b, i: (b, 0, 0)),
                  pl.BlockSpec((1, mlen, d), lambda b, i: (b, 0, 0))],
        out_specs=pl.BlockSpec((tq, d), lambda b, i: (b * nq + i, 0)),
        out_shape=jax.ShapeDtypeStruct((n, d), BF16),
        compiler_params=pltpu.CompilerParams(
            dimension_semantics=("arbitrary", "arbitrary"),
            vmem_limit_bytes=_vmem_limit(resident)),
        name="mem_attn",
    )(q, mem_k, mem_v)


def _ffn_kernel(x_ref, g_ref, wg_ref, wu_ref, wd_ref, cw_ref, cb_ref, past_ref, gf_ref,
                y_ref, conv_ref, hn_ref, acc_ref, carry_ref, *, tm, blocks_per_batch):
    i, j = pl.program_id(0), pl.program_id(1)

    @pl.when((i == 0) & (j == 0))
    def _():
        carry_ref[...] = jnp.zeros(carry_ref.shape, F32)

    @pl.when(j == 0)
    def _():
        hn_ref[...] = _rms(x_ref[...], g_ref[...]).astype(BF16)
        acc_ref[...] = jnp.zeros(acc_ref.shape, F32)

    hn = hn_ref[...]
    g = jnp.dot(hn, wg_ref[...], preferred_element_type=F32)
    u = jnp.dot(hn, wu_ref[...], preferred_element_type=F32)

    first = (i % blocks_per_batch) == 0
    prev = jnp.where(first, past_ref[0], carry_ref[j])
    tail = g[tm - (CONV_W - 1):tm]
    carry_ref[j] = tail
    conv_ref[0] = tail

    row = lax.broadcasted_iota(jnp.int32, g.shape, 0)
    g1 = jnp.where(row == 0, prev[1:2], pltpu.roll(g, 1, 0))
    g2 = jnp.where(row == 0, prev[0:1], jnp.where(row == 1, prev[1:2], pltpu.roll(g, 2, 0)))
    cw = cw_ref[...]
    c = cb_ref[...] + cw[0:1] * g2 + cw[1:2] * g1 + cw[2:3] * g
    act = (c * jax.nn.sigmoid(c)) * u
    acc_ref[...] += jnp.dot(act.astype(BF16), wd_ref[...], preferred_element_type=F32)

    @pl.when(j == pl.num_programs(1) - 1)
    def _():
        y_ref[...] = _rms(x_ref[...] + acc_ref[...], gf_ref[...])


def _ffn(x, g_ffn, wg, wu, wd, conv_w, conv_b, past, g_final, batch, tm, tf):
    n, d = x.shape
    dff = wg.shape[1]
    nj = dff // tf
    bpb = n // batch // tm
    resident = (4 * _nbytes((tm, d), F32) + _nbytes((tm, d), BF16) + _nbytes((tm, d), F32)
                + 6 * _nbytes((d, tf), BF16) + 8 * _nbytes((tm, tf), F32) + 2 * _nbytes((2, dff), F32))
    y, conv = pl.pallas_call(
        functools.partial(_ffn_kernel, tm=tm, blocks_per_batch=bpb),
        grid=(n // tm, nj),
        in_specs=[pl.BlockSpec((tm, d), lambda i, j: (i, 0)),
                  pl.BlockSpec((1, d), lambda i, j: (0, 0)),
                  pl.BlockSpec((d, tf), lambda i, j: (0, j)),
                  pl.BlockSpec((d, tf), lambda i, j: (0, j)),
                  pl.BlockSpec((tf, d), lambda i, j: (j, 0)),
                  pl.BlockSpec((CONV_W, tf), lambda i, j: (0, j)),
                  pl.BlockSpec((1, tf), lambda i, j: (0, j)),
                  pl.BlockSpec((1, CONV_W - 1, tf), lambda i, j: (i // bpb, 0, j)),
                  pl.BlockSpec((1, d), lambda i, j: (0, 0))],
        out_specs=[pl.BlockSpec((tm, d), lambda i, j: (i, 0)),
                   pl.BlockSpec((1, CONV_W - 1, tf), lambda i, j: (i, 0, j))],
        out_shape=[jax.ShapeDtypeStruct((n, d), F32),
                   jax.ShapeDtypeStruct((n // tm, CONV_W - 1, dff), F32)],
        scratch_shapes=[pltpu.VMEM((tm, d), BF16), pltpu.VMEM((tm, d), F32),
                        pltpu.VMEM((nj, CONV_W - 1, tf), F32)],
        compiler_params=pltpu.CompilerParams(
            dimension_semantics=("arbitrary", "arbitrary"),
            vmem_limit_bytes=_vmem_limit(resident)),
        name="conv_ffn",
    )(x, g_ffn, wg, wu, wd, conv_w, conv_b, past, g_final)
    return y, conv[bpb - 1::bpb]


def _pad_cols(a, width):
    return jnp.pad(a, ((0, 0), (0, width - a.shape[1])))


def _layer_pass(x, wts, batch, seq, da_cache, ml_state, conv_past, mem_kv, lam_init, ml_chunk, tiles):
    n, d = x.shape
    tm = tiles["tm"]
    width = wts["da_width"]
    plan = [(0, BF16, DA_QSCALE), (1, F32, None), (1, BF16, None), (2, F32, None), (2, BF16, None),
            (3, BF16, None), (4, BF16, None), (5, BF16, None), (6, BF16, None)]
    (dq, k_f32, dk, v_f32, dv, mq, mk, mv, mo, gates) = _norm_proj(
        x, wts["g_mix"], wts["w_in"], 7, width, plan, tm, tiles["tn_in"],
        gate_w=(wts["w_gate_hi"], wts["w_gate_lo"]), name="in_proj")

    if da_cache is None:
        da_out = _da_prompt(dq, dk, dv, wts["lams"], wts["g_da"], batch, seq, lam_init, tiles["tq"])
    else:
        da_out = _da_sample(dq, dk, dv, da_cache[0], da_cache[1], wts["lams"], wts["g_da"], lam_init,
                            tiles["tk_cache"])

    ml_out, c_new, n_new, m_new = _mlstm(mq, mk, mv, mo, gates, wts["gate_bias"], wts["g_ml"],
                                         ml_state[0], ml_state[1], ml_state[2], batch, ml_chunk)

    x1 = _matmul_res(da_out, 0, ml_out, 0, wts["w_out"], x, tm, tiles["tn"], "out_proj")

    (mq_x,) = _norm_proj(x1, wts["g_xattn"], wts["w_mq"], 1, d, [(0, BF16, None)], tm, tiles["tn"],
                         name="memq_proj")
    mo_x = _mem_attn(mq_x, mem_kv[0], mem_kv[1], batch, min(tm, seq))
    x2 = _matmul_res(mo_x, 0, mo_x, 1, wts["w_mo"], x1, tm, tiles["tn"], "memo_proj")

    y, conv_new = _ffn(x2, wts["g_ffn"], wts["w_ffn_gate"], wts["w_ffn_up"], wts["w_ffn_down"],
                       wts["conv_w"], wts["conv_b"], conv_past, wts["g_final"], batch,
                       tiles["tm_ffn"], tiles["tf"])
    return y, k_f32, v_f32, (c_new, n_new, m_new), conv_new


def kernel(x_prompt, x_sample, cache_da_k, cache_da_v, state_ml_c, state_ml_n, state_ml_m, state_ffn_conv,
           cache_mem_k, cache_mem_v, mem_prompt, w_in, g_mix, lambda_q1, lambda_k1, lambda_q2, lambda_k2,
           g_da_sub, b_ig, b_fg, g_ml, w_out, g_xattn, g_mem, w_mq, w_mk, w_mv, w_mo, g_ffn, w_gate, w_up,
           conv_w, conv_b, w_down, g_final):
    depth = w_in.shape[0]
    assert depth == 1, "single-layer encoder"
    batch, seq, d = x_prompt.shape
    dbatch, dseq, _ = x_sample.shape
    past = cache_da_k.shape[2]
    mlen = mem_prompt.shape[1]
    dff = w_gate.shape[2]
    da_width = d // 2
    ml_width = d - da_width
    hd = ml_width // ML_HEADS
    assert seq % 512 == 0 and past % 1024 == 0 and past % CHUNK == 0 and dseq <= CHUNK
    tf = 512
    dff_p = -(-dff // tf) * tf
    l = 0
    lam_init = 0.8 - 0.6 * math.exp(-0.3 * l)

    n_main = 3 * da_width + 4 * ml_width
    w_in_l = w_in[l]
    w_g = _pad_cols(w_in_l[:, n_main:], LANES)
    w_g_hi = w_g.astype(BF16)
    wts = dict(
        da_width=da_width,
        w_in=w_in_l[:, :n_main].astype(BF16),
        w_gate_hi=w_g_hi,
        w_gate_lo=(w_g - w_g_hi.astype(F32)).astype(BF16),
        gate_bias=_pad_cols(jnp.concatenate([b_ig[l], b_fg[l]])[None, :], LANES),
        g_mix=g_mix[l], g_xattn=g_xattn[l], g_ffn=g_ffn[l].reshape(1, d), g_final=g_final.reshape(1, d),
        lams=jnp.stack([lambda_q1[l], lambda_k1[l], lambda_q2[l], lambda_k2[l]]),
        g_da=g_da_sub[l].reshape(1, 2 * DA_HEAD_DIM),
        g_ml=g_ml[l].reshape(1, ml_width),
        w_out=w_out[l].astype(BF16), w_mq=w_mq[l].astype(BF16), w_mo=w_mo[l].astype(BF16),
        w_ffn_gate=_pad_cols(w_gate[l], dff_p).astype(BF16),
        w_ffn_up=_pad_cols(w_up[l], dff_p).astype(BF16),
        w_ffn_down=jnp.pad(w_down[l], ((0, dff_p - dff), (0, 0))).astype(BF16),
        conv_w=_pad_cols(conv_w[l], dff_p),
        conv_b=_pad_cols(conv_b[l][None, :], dff_p),
    )

    w_mkv = jnp.concatenate([w_mk[l], w_mv[l]], axis=1).astype(BF16)
    mk_f32, mk_bf, mv_f32, mv_bf = _norm_proj(
        mem_prompt.reshape(batch * mlen, d), g_mem[l], w_mkv, 2, d,
        [(0, F32, None), (0, BF16, None), (1, F32, None), (1, BF16, None)], 512, 512, name="memkv_proj")
    zero_state = (jnp.zeros((batch, ML_HEADS, hd, hd), F32), jnp.zeros((batch, ML_HEADS, hd), F32),
                  jnp.zeros((batch, ML_HEADS), F32))
    tiles_p = dict(tm=512, tn_in=256, tn=d, tq=512, tm_ffn=512, tf=tf)
    yp, pk, pv, (pc, pn, pm), pconv = _layer_pass(
        x_prompt.reshape(batch * seq, d), wts, batch, seq, None, zero_state,
        jnp.zeros((batch, CONV_W - 1, dff_p), F32),
        (mk_bf.reshape(batch, mlen, d), mv_bf.reshape(batch, mlen, d)), lam_init, CHUNK, tiles_p)

    ns = dbatch * dseq
    tiles_s = dict(tm=ns, tn_in=256, tn=d, tk_cache=1024, tm_ffn=dseq, tf=tf)
    ys, sk, sv, (sc, sn, sm), sconv = _layer_pass(
        x_sample.reshape(ns, d), wts, dbatch, dseq,
        (cache_da_k.reshape(dbatch, past, da_width), cache_da_v.reshape(dbatch, past, da_width)),
        (state_ml_c[l], state_ml_n[l], state_ml_m[l]),
        _pad_cols(state_ffn_conv[l].reshape(dbatch * (CONV_W - 1), dff), dff_p).reshape(dbatch, CONV_W - 1, dff_p),
        (cache_mem_k[l].reshape(dbatch, mlen, d).astype(BF16), cache_mem_v[l].reshape(dbatch, mlen, d).astype(BF16)),
        lam_init, dseq, tiles_s)

    kv_shape_p = (1, batch, seq, DA_HEADS, 2 * DA_HEAD_DIM)
    kv_shape_s = (1, dbatch, dseq, DA_HEADS, 2 * DA_HEAD_DIM)
    mem_shape = (1, batch, mlen, MEM_HEADS, d // MEM_HEADS)
    return (yp.reshape(batch, seq, d), ys.reshape(dbatch, dseq, d),
            pk.reshape(kv_shape_p), pv.reshape(kv_shape_p), pc[None], pn[None], pm[None],
            pconv[None, :, :, :dff], mk_f32.reshape(mem_shape), mv_f32.reshape(mem_shape),
            sk.reshape(kv_shape_s), sv.reshape(kv_shape_s), sc[None], sn[None], sm[None],
            sconv[None, :, :, :dff])
```

```python
import functools
import math

import jax
import jax.numpy as jnp
from jax import lax
from jax.experimental import pallas as pl
from jax.experimental.pallas import tpu as pltpu

F32 = jnp.float32
BF16 = jnp.bfloat16

EPS = 1e-6
NEG = -1e30
CHUNK = 64
DA_HEADS = 8
DA_HEAD_DIM = 64
ML_HEADS = 4
MEM_HEADS = 4
CONV_W = 3
DA_QSCALE = (DA_HEAD_DIM ** -0.5) * math.log2(math.e)

LANES = 128
V7X_VMEM_BYTES = 64 * 1024 * 1024
VMEM_CEILING = V7X_VMEM_BYTES - 8 * 1024 * 1024


def _vmem_limit(block_bytes):
    return int(min(VMEM_CEILING, block_bytes * 1.25 + 8 * 1024 * 1024))


def _nbytes(shape, dtype):
    return math.prod(shape) * jnp.dtype(dtype).itemsize


def _rms(x, g):
    return x * lax.rsqrt(jnp.mean(x * x, axis=-1, keepdims=True) + EPS) * g


def _proj_kernel(*refs, n_pieces, out_plan, has_gate):
    x_ref, g_ref = refs[0], refs[1]
    w_refs = refs[2:2 + n_pieces]
    pos = 2 + n_pieces
    if has_gate:
        wgh_ref, wgl_ref = refs[pos], refs[pos + 1]
        pos += 2
    out_refs = refs[pos:pos + len(out_plan)]
    pos += len(out_plan)
    if has_gate:
        gate_ref = refs[pos]
        pos += 1
    xn_ref = refs[pos]

    @pl.when(pl.program_id(1) == 0)
    def _():
        xn = _rms(x_ref[...], g_ref[...])
        hi = xn.astype(BF16)
        xn_ref[...] = hi
        if has_gate:
            lo = (xn - hi.astype(F32)).astype(BF16)
            gate_ref[...] = (jnp.dot(hi, wgh_ref[...], preferred_element_type=F32)
                             + jnp.dot(lo, wgh_ref[...], preferred_element_type=F32)
                             + jnp.dot(hi, wgl_ref[...], preferred_element_type=F32))

    xn = xn_ref[...]
    for p in range(n_pieces):
        acc = jnp.dot(xn, w_refs[p][...], preferred_element_type=F32)
        for (piece, _, scale), o_ref in zip(out_plan, out_refs):
            if piece == p:
                o_ref[...] = (acc if scale is None else acc * scale).astype(o_ref.dtype)


def _norm_proj(x, g, w, n_pieces, width, out_plan, tm, tn, gate_w=None, name="proj"):
    n, d = x.shape
    nj = width // tn
    in_specs = [pl.BlockSpec((tm, d), lambda i, j: (i, 0)),
                pl.BlockSpec((1, d), lambda i, j: (0, 0))]
    args = [x, g.reshape(1, d)]
    for p in range(n_pieces):
        in_specs.append(pl.BlockSpec((d, tn), functools.partial(lambda i, j, p: (0, p * nj + j), p=p)))
        args.append(w)
    has_gate = gate_w is not None
    if has_gate:
        for gw in gate_w:
            in_specs.append(pl.BlockSpec((d, LANES), lambda i, j: (0, 0)))
            args.append(gw)
    out_shape = [jax.ShapeDtypeStruct((n, width), dt) for _, dt, _ in out_plan]
    out_specs = [pl.BlockSpec((tm, tn), lambda i, j: (i, j)) for _ in out_plan]
    if has_gate:
        out_shape.append(jax.ShapeDtypeStruct((n, LANES), F32))
        out_specs.append(pl.BlockSpec((tm, LANES), lambda i, j: (i, 0)))
    resident = (2 * _nbytes((tm, d), F32) + _nbytes((tm, d), BF16)
                + 2 * n_pieces * _nbytes((d, tn), BF16)
                + 2 * sum(_nbytes((tm, tn), dt) for _, dt, _ in out_plan)
                + (4 * _nbytes((d, LANES), BF16) + 2 * _nbytes((tm, LANES), F32) if has_gate else 0)
                + 2 * _nbytes((tm, tn), F32))
    return pl.pallas_call(
        functools.partial(_proj_kernel, n_pieces=n_pieces, out_plan=tuple(out_plan), has_gate=has_gate),
        grid=(n // tm, nj),
        in_specs=in_specs,
        out_specs=out_specs,
        out_shape=out_shape,
        scratch_shapes=[pltpu.VMEM((tm, d), BF16)],
        compiler_params=pltpu.CompilerParams(
            dimension_semantics=("arbitrary", "arbitrary"),
            vmem_limit_bytes=_vmem_limit(resident)),
        name=name,
    )(*args)


def _da_update(q, k, v, mask, m_ref, l_ref, acc_ref, h):
    lane = lax.broadcasted_iota(jnp.int32, k.shape, 1)
    zero = jnp.zeros_like(k)
    for c in range(2):
        kc = jnp.where((lane < DA_HEAD_DIM) if c == 0 else (lane >= DA_HEAD_DIM), k, zero)
        s = lax.dot_general(q, kc, (((1,), (1,)), ((), ())), preferred_element_type=F32)
        if mask is not None:
            s = jnp.where(mask, s, NEG)
        idx = 2 * h + c
        m_prev = m_ref[idx]
        m_new = jnp.maximum(m_prev, jnp.max(s, axis=-1, keepdims=True))
        p = jnp.exp2(s - m_new)
        alpha = jnp.exp2(m_prev - m_new)
        l_ref[idx] = alpha * l_ref[idx] + jnp.sum(p, axis=-1, keepdims=True)
        m_ref[idx] = m_new
        pv = jnp.dot(p.astype(BF16), v, preferred_element_type=F32)
        sl = slice(idx * LANES, (idx + 1) * LANES)
        acc_ref[:, sl] = alpha * acc_ref[:, sl] + pv


def _da_init(m_ref, l_ref, acc_ref):
    m_ref[...] = jnp.full(m_ref.shape, NEG, F32)
    l_ref[...] = jnp.zeros(l_ref.shape, F32)
    acc_ref[...] = jnp.zeros(acc_ref.shape, F32)


def _da_finalize(lams_ref, gda_ref, m_ref, l_ref, acc_ref, o_ref, lam_init):
    lam = _da_lambda(lams_ref, lam_init)
    gda = gda_ref[...]
    for h in range(DA_HEADS):
        o0 = acc_ref[:, (2 * h) * LANES:(2 * h + 1) * LANES] / l_ref[2 * h]
        o1 = acc_ref[:, (2 * h + 1) * LANES:(2 * h + 2) * LANES] / l_ref[2 * h + 1]
        o = o0 - lam * o1
        o_ref[:, h * LANES:(h + 1) * LANES] = (_rms(o, gda) * (1.0 - lam_init)).astype(o_ref.dtype)


def _chunk_mask(q0, k0, tq, tk):
    q_pos = q0 + lax.broadcasted_iota(jnp.int32, (tq, tk), 0)
    k_pos = k0 + lax.broadcasted_iota(jnp.int32, (tq, tk), 1)
    return (k_pos // CHUNK) <= (q_pos // CHUNK)


def _da_lambda(lams_ref, lam_init):
    lams = lams_ref[...]
    return (jnp.exp(jnp.sum(lams[0:1] * lams[1:2], axis=-1, keepdims=True))
            - jnp.exp(jnp.sum(lams[2:3] * lams[3:4], axis=-1, keepdims=True)) + lam_init)


def _da_prompt_kernel(lams_ref, gda_ref, q_ref, k_ref, v_ref, o_ref, qc_ref, m_ref, l_ref, acc_ref, *, tq, lam_init):
    qi, ki = pl.program_id(1), pl.program_id(2)

    @pl.when(ki == 0)
    def _():
        m_ref[...] = jnp.full(m_ref.shape, NEG, F32)
        l_ref[...] = jnp.zeros(l_ref.shape, F32)
        acc_ref[...] = jnp.zeros(acc_ref.shape, F32)
        qs = q_ref[...]
        lane = lax.broadcasted_iota(jnp.int32, qs.shape, 1) % LANES
        zero = jnp.zeros_like(qs)
        qc_ref[0] = jnp.where(lane < DA_HEAD_DIM, qs, zero)
        qc_ref[1] = jnp.where(lane >= DA_HEAD_DIM, qs, zero)

    def scores(idx):
        h, c = divmod(idx, 2)
        sl = slice(h * LANES, (h + 1) * LANES)
        return lax.dot_general(k_ref[:, sl], qc_ref[c, :, sl], (((1,), (1,)), ((), ())),
                               preferred_element_type=F32)

    def key_block(mask):
        st_next = scores(0)
        for idx in range(2 * DA_HEADS):
            st = st_next
            if idx + 1 < 2 * DA_HEADS:
                st_next = scores(idx + 1)
            if mask is not None:
                st = jnp.where(mask, st, NEG)
            m_prev = m_ref[idx]
            m_new = jnp.maximum(m_prev, jnp.max(st, axis=0, keepdims=True))
            p = jnp.exp2(st - m_new)
            alpha = jnp.exp2(m_prev - m_new)
            l_ref[idx] = alpha * l_ref[idx] + jnp.sum(p, axis=0, keepdims=True)
            m_ref[idx] = m_new
            sl = slice((idx // 2) * LANES, (idx // 2 + 1) * LANES)
            pv = lax.dot_general(v_ref[:, sl], p.astype(BF16), (((0,), (0,)), ((), ())),
                                 preferred_element_type=F32)
            acc_ref[idx] = alpha * acc_ref[idx] + pv

    @pl.when(ki < qi)
    def _():
        key_block(None)

    @pl.when(ki == qi)
    def _():
        k_pos = lax.broadcasted_iota(jnp.int32, (tq, tq), 0)
        q_pos = lax.broadcasted_iota(jnp.int32, (tq, tq), 1)
        key_block((k_pos // CHUNK) <= (q_pos // CHUNK))
        lam = _da_lambda(lams_ref, lam_init)
        gda = gda_ref[...]
        for h in range(DA_HEADS):
            o0 = acc_ref[2 * h] * (1.0 / l_ref[2 * h])
            o1 = acc_ref[2 * h + 1] * (1.0 / l_ref[2 * h + 1])
            o = o0 - lam * o1
            y = o * lax.rsqrt(jnp.mean(o * o, axis=0, keepdims=True) + EPS) * gda * (1.0 - lam_init)
            o_ref[:, h * LANES:(h + 1) * LANES] = y.T.astype(o_ref.dtype)


def _da_prompt(q, k, v, lams, g_da, batch, seq, lam_init, tq):
    n, w = q.shape
    nq = seq // tq
    qmap = lambda b, qi, ki: (b * nq + qi, 0)
    kmap = lambda b, qi, ki: (b * nq + jnp.minimum(ki, qi), 0)
    const = lambda b, qi, ki: (0, 0)
    g_col = g_da.reshape(LANES, 1)
    resident = (8 * _nbytes((tq, w), BF16) + 2 * _nbytes((tq, w), BF16) + _nbytes((tq, 2 * w), F32)
                + 8 * _nbytes((tq, tq), F32))
    return pl.pallas_call(
        functools.partial(_da_prompt_kernel, tq=tq, lam_init=lam_init),
        grid=(batch, nq, nq),
        in_specs=[pl.BlockSpec(lams.shape, const), pl.BlockSpec(g_col.shape, const),
                  pl.BlockSpec((tq, w), qmap), pl.BlockSpec((tq, w), kmap), pl.BlockSpec((tq, w), kmap)],
        out_specs=pl.BlockSpec((tq, w), qmap),
        out_shape=jax.ShapeDtypeStruct((n, w), BF16),
        scratch_shapes=[pltpu.VMEM((2, tq, w), BF16),
                        pltpu.VMEM((2 * DA_HEADS, 1, tq), F32), pltpu.VMEM((2 * DA_HEADS, 1, tq), F32),
                        pltpu.VMEM((2 * DA_HEADS, LANES, tq), F32)],
        compiler_params=pltpu.CompilerParams(
            dimension_semantics=("arbitrary", "arbitrary", "arbitrary"),
            vmem_limit_bytes=_vmem_limit(resident)),
        name="da_prompt",
    )(lams, g_col, q, k, v)


def _da_sample_kernel(lams_ref, gda_ref, q_ref, kn_ref, vn_ref, ck_ref, cv_ref, o_ref, m_ref, l_ref, acc_ref,
                      *, t, tk, past, lam_init):
    ki = pl.program_id(1)

    @pl.when(ki == 0)
    def _():
        _da_init(m_ref, l_ref, acc_ref)

    mask = _chunk_mask(past, ki * tk, t, tk)
    for h in range(DA_HEADS):
        sl = slice(h * LANES, (h + 1) * LANES)
        kblk = ck_ref[0, 0, :, h, :].astype(BF16)
        vblk = cv_ref[0, 0, :, h, :].astype(BF16)
        _da_update(q_ref[:, sl], kblk, vblk, mask, m_ref, l_ref, acc_ref, h)

    @pl.when(ki == pl.num_programs(1) - 1)
    def _():
        mask_new = _chunk_mask(past, past, t, t)
        for h in range(DA_HEADS):
            sl = slice(h * LANES, (h + 1) * LANES)
            _da_update(q_ref[:, sl], kn_ref[:, sl], vn_ref[:, sl], mask_new, m_ref, l_ref, acc_ref, h)
        _da_finalize(lams_ref, gda_ref, m_ref, l_ref, acc_ref, o_ref, lam_init)


def _da_sample(q, kn, vn, cache_k, cache_v, lams, g_da, lam_init, tk):
    n, w = q.shape
    _, batch, past, _, _ = cache_k.shape
    t = n // batch
    const = lambda b, ki: (0, 0)
    rows = lambda b, ki: (b, 0)
    cmap = lambda b, ki: (0, b, ki, 0, 0)
    cblock = (1, 1, tk, DA_HEADS, LANES)
    resident = (4 * _nbytes((tk, w), F32) + 2 * _nbytes((tk, w), BF16) + 8 * _nbytes((t, w), BF16)
                + _nbytes((t, 2 * w), F32) + 4 * DA_HEADS * _nbytes((t, LANES), F32))
    return pl.pallas_call(
        functools.partial(_da_sample_kernel, t=t, tk=tk, past=past, lam_init=lam_init),
        grid=(batch, past // tk),
        in_specs=[pl.BlockSpec(lams.shape, const), pl.BlockSpec(g_da.shape, const),
                  pl.BlockSpec((t, w), rows), pl.BlockSpec((t, w), rows), pl.BlockSpec((t, w), rows),
                  pl.BlockSpec(cblock, cmap), pl.BlockSpec(cblock, cmap)],
        out_specs=pl.BlockSpec((t, w), rows),
        out_shape=jax.ShapeDtypeStruct((n, w), BF16),
        scratch_shapes=[pltpu.VMEM((2 * DA_HEADS, t, 1), F32), pltpu.VMEM((2 * DA_HEADS, t, 1), F32),
                        pltpu.VMEM((t, 2 * w), F32)],
        compiler_params=pltpu.CompilerParams(
            dimension_semantics=("arbitrary", "arbitrary"),
            vmem_limit_bytes=_vmem_limit(resident)),
        name="da_sample",
    )(lams, g_da, q, kn, vn, cache_k, cache_v)


def _ml_kernel(q_ref, k_ref, v_ref, og_ref, gate_ref, bias_ref, gml_ref, c0_ref, n0_ref, m0_ref,
               out_ref, cout_ref, nout_ref, mout_ref, c_s, n_s, m_s, *, L, hd):
    ci = pl.program_id(1)

    @pl.when(ci == 0)
    def _():
        c_s[...] = c0_ref[0]
        n_s[...] = n0_ref[0]
        m_s[...] = m0_ref[0]

    gates = gate_ref[...] + bias_ref[...]
    glane = lax.broadcasted_iota(jnp.int32, gates.shape, 1)
    row = lax.broadcasted_iota(jnp.int32, (L, L), 0)
    col = lax.broadcasted_iota(jnp.int32, (L, L), 1)
    tri = col <= row
    eye = col == row
    m_all = m_s[...]
    mlane = lax.broadcasted_iota(jnp.int32, m_all.shape, 1)
    m_next = jnp.zeros_like(m_all)

    def lane_pick(x, lanes, idx):
        return jnp.sum(jnp.where(lanes == idx, x, 0.0), axis=-1, keepdims=True)

    def to_row(x_col):
        return jnp.sum(jnp.where(eye, x_col, 0.0), axis=0, keepdims=True)

    for h in range(ML_HEADS):
        sl = slice(h * hd, (h + 1) * hd)
        ic = lane_pick(gates, glane, h)
        lf = jax.nn.log_sigmoid(lane_pick(gates, glane, ML_HEADS + h))
        m_prev = lane_pick(m_all, mlane, h)
        ic_row = to_row(ic)
        b_col = jnp.sum(jnp.where(tri, to_row(lf), 0.0), axis=-1, keepdims=True)
        b_row = to_row(b_col)
        dmat = jnp.where(tri, b_col - b_row + ic_row, -jnp.inf)
        inter = b_col + m_prev
        m_t = jnp.maximum(inter, jnp.max(dmat, axis=-1, keepdims=True))
        w = jnp.exp(dmat - m_t)
        g = jnp.exp(inter - m_t)

        qc = q_ref[:, sl]
        kc = k_ref[:, sl] * (hd ** -0.5)
        vc = v_ref[:, sl]
        c_old = c_s[h]
        n_old = n_s[h]
        s = lax.dot_general(qc, kc, (((1,), (1,)), ((), ())), preferred_element_type=F32) * w
        num = (jnp.dot(s.astype(BF16), vc, preferred_element_type=F32)
               + g * lax.dot_general(qc, c_old.astype(BF16), (((1,), (1,)), ((), ())),
                                     preferred_element_type=F32))
        den = (jnp.sum(s, axis=-1, keepdims=True)
               + g * jnp.sum(qc.astype(F32) * n_old, axis=-1, keepdims=True))
        hm = num / jnp.maximum(jnp.abs(den), jnp.exp(-m_t))

        m_new = m_t[L - 1:L]
        wl = jnp.exp(b_col[L - 1:L] - b_col + ic - m_new)
        gl = jnp.exp(inter[L - 1:L] - m_new)
        kf = kc.astype(F32)
        vw = (vc.astype(F32) * wl).astype(BF16)
        c_s[h] = gl * c_old + lax.dot_general(vw, kc, (((0,), (0,)), ((), ())), preferred_element_type=F32)
        n_s[h] = gl * n_old + jnp.sum(wl * kf, axis=0, keepdims=True)
        m_next = m_next + jnp.where(mlane == h, m_new, 0.0)

        y = _rms(hm, gml_ref[:, sl]) * jax.nn.sigmoid(og_ref[:, sl].astype(F32))
        out_ref[:, sl] = y.astype(out_ref.dtype)

    m_s[...] = m_next

    @pl.when(ci == pl.num_programs(1) - 1)
    def _():
        cout_ref[0] = c_s[...]
        nout_ref[0] = n_s[...]
        mout_ref[0] = m_s[...]


def _mlstm(mq, mk, mv, mo, gates, bias, g_ml, c0, n0, m0, batch, L):
    n, w = mq.shape
    hd = w // ML_HEADS
    nc = n // batch // L
    rows = lambda b, c: (b * nc + c, 0)
    const = lambda b, c: (0, 0)
    st4 = lambda b, c: (b, 0, 0, 0)
    st3 = lambda b, c: (b, 0, 0)
    n0 = n0.reshape(batch, ML_HEADS, 1, hd)
    m0 = jnp.pad(m0, ((0, 0), (0, LANES - ML_HEADS))).reshape(batch, 1, LANES)
    resident = (4 * _nbytes((1, ML_HEADS, hd, hd), F32) + _nbytes((ML_HEADS, hd, hd), F32)
                + 10 * _nbytes((L, w), BF16) + 4 * _nbytes((L, LANES), F32) + 8 * _nbytes((hd, hd), F32))
    out, c, nn, m = pl.pallas_call(
        functools.partial(_ml_kernel, L=L, hd=hd),
        grid=(batch, nc),
        in_specs=[pl.BlockSpec((L, w), rows), pl.BlockSpec((L, w), rows), pl.BlockSpec((L, w), rows),
                  pl.BlockSpec((L, w), rows), pl.BlockSpec((L, LANES), rows),
                  pl.BlockSpec((1, LANES), const), pl.BlockSpec((1, w), const),
                  pl.BlockSpec((1, ML_HEADS, hd, hd), st4), pl.BlockSpec((1, ML_HEADS, 1, hd), st4),
                  pl.BlockSpec((1, 1, LANES), st3)],
        out_specs=[pl.BlockSpec((L, w), rows), pl.BlockSpec((1, ML_HEADS, hd, hd), st4),
                   pl.BlockSpec((1, ML_HEADS, 1, hd), st4), pl.BlockSpec((1, 1, LANES), st3)],
        out_shape=[jax.ShapeDtypeStruct((n, w), BF16), jax.ShapeDtypeStruct((batch, ML_HEADS, hd, hd), F32),
                   jax.ShapeDtypeStruct((batch, ML_HEADS, 1, hd), F32),
                   jax.ShapeDtypeStruct((batch, 1, LANES), F32)],
        scratch_shapes=[pltpu.VMEM((ML_HEADS, hd, hd), F32), pltpu.VMEM((ML_HEADS, 1, hd), F32),
                        pltpu.VMEM((1, LANES), F32)],
        compiler_params=pltpu.CompilerParams(
            dimension_semantics=("arbitrary", "arbitrary"),
            vmem_limit_bytes=_vmem_limit(resident)),
        name="mlstm",
    )(mq, mk, mv, mo, gates, bias, g_ml, c0, n0, m0)
    return out, c, nn.reshape(batch, ML_HEADS, hd), m[:, 0, :ML_HEADS]


def _matmul_res_kernel(a_ref, b_ref, wa_ref, wb_ref, r_ref, o_ref):
    o_ref[...] = (r_ref[...] + jnp.dot(a_ref[...], wa_ref[...], preferred_element_type=F32)
                  + jnp.dot(b_ref[...], wb_ref[...], preferred_element_type=F32))


def _matmul_res(a, a_blk, b, b_blk, w, resid, tm, tn, name):
    n, dout = resid.shape
    kh = w.shape[0] // 2
    resident = 2 * (2 * _nbytes((tm, kh), BF16) + 2 * _nbytes((kh, tn), BF16) + 3 * _nbytes((tm, tn), F32))
    return pl.pallas_call(
        _matmul_res_kernel,
        grid=(n // tm, dout // tn),
        in_specs=[pl.BlockSpec((tm, kh), lambda i, j: (i, a_blk)),
                  pl.BlockSpec((tm, kh), lambda i, j: (i, b_blk)),
                  pl.BlockSpec((kh, tn), lambda i, j: (0, j)),
                  pl.BlockSpec((kh, tn), lambda i, j: (1, j)),
                  pl.BlockSpec((tm, tn), lambda i, j: (i, j))],
        out_specs=pl.BlockSpec((tm, tn), lambda i, j: (i, j)),
        out_shape=jax.ShapeDtypeStruct((n, dout), F32),
        compiler_params=pltpu.CompilerParams(
            dimension_semantics=("arbitrary", "arbitrary"),
            vmem_limit_bytes=_vmem_limit(resident)),
        name=name,
    )(a, b, w, w, resid)


def _mem_attn_kernel(q_ref, k_ref, v_ref, o_ref, *, hd):
    for h in range(MEM_HEADS):
        sl = slice(h * hd, (h + 1) * hd)
        s = lax.dot_general(q_ref[:, sl], k_ref[0, :, sl], (((1,), (1,)), ((), ())),
                            preferred_element_type=F32) * (hd ** -0.5)
        p = jnp.exp(s - jnp.max(s, axis=-1, keepdims=True))
        a = p / jnp.sum(p, axis=-1, keepdims=True)
        o_ref[:, sl] = jnp.dot(a.astype(BF16), v_ref[0, :, sl], preferred_element_type=F32).astype(o_ref.dtype)


def _mem_attn(q, mem_k, mem_v, batch, tq):
    n, d = q.shape
    mlen = mem_k.shape[1]
    nq = n // batch // tq
    resident = 4 * _nbytes((tq, d), BF16) + 4 * _nbytes((mlen, d), BF16) + 6 * _nbytes((tq, mlen), F32)
    return pl.pallas_call(
        functools.partial(_mem_attn_kernel, hd=d // MEM_HEADS),
        grid=(batch, nq),
        in_specs=[pl.BlockSpec((tq, d), lambda b, i: (b * nq + i, 0)),
                  pl.BlockSpec((1, mlen, d), lambda b, i: (b, 0, 0)),
                  pl.BlockSpec((1, mlen, d), lambda b, i: (b, 0, 0))],
        out_specs=pl.BlockSpec((tq, d), lambda b, i: (b * nq + i, 0)),
        out_shape=jax.ShapeDtypeStruct((n, d), BF16),
        compiler_params=pltpu.CompilerParams(
            dimension_semantics=("arbitrary", "arbitrary"),
            vmem_limit_bytes=_vmem_limit(resident)),
        name="mem_attn",
    )(q, mem_k, mem_v)


def _ffn_kernel(x_ref, g_ref, wg_ref, wu_ref, wd_ref, cw_ref, cb_ref, past_ref, gf_ref,
                y_ref, conv_ref, hn_ref, acc_ref, carry_ref, *, tm, blocks_per_batch):
    i, j = pl.program_id(0), pl.program_id(1)

    @pl.when((i == 0) & (j == 0))
    def _():
        carry_ref[...] = jnp.zeros(carry_ref.shape, F32)

    @pl.when(j == 0)
    def _():
        hn_ref[...] = _rms(x_ref[...], g_ref[...]).astype(BF16)
        acc_ref[...] = jnp.zeros(acc_ref.shape, F32)

    hn = hn_ref[...]
    g = jnp.dot(hn, wg_ref[...], preferred_element_type=F32)
    u = jnp.dot(hn, wu_ref[...], preferred_element_type=F32)

    first = (i % blocks_per_batch) == 0
    prev = jnp.where(first, past_ref[0], carry_ref[j])
    tail = g[tm - (CONV_W - 1):tm]
    carry_ref[j] = tail
    conv_ref[0] = tail

    row = lax.broadcasted_iota(jnp.int32, g.shape, 0)
    g1 = jnp.where(row == 0, prev[1:2], pltpu.roll(g, 1, 0))
    g2 = jnp.where(row == 0, prev[0:1], jnp.where(row == 1, prev[1:2], pltpu.roll(g, 2, 0)))
    cw = cw_ref[...]
    c = cb_ref[...] + cw[0:1] * g2 + cw[1:2] * g1 + cw[2:3] * g
    act = (c * jax.nn.sigmoid(c)) * u
    acc_ref[...] += jnp.dot(act.astype(BF16), wd_ref[...], preferred_element_type=F32)

    @pl.when(j == pl.num_programs(1) - 1)
    def _():
        y_ref[...] = _rms(x_ref[...] + acc_ref[...], gf_ref[...])


def _ffn(x, g_ffn, wg, wu, wd, conv_w, conv_b, past, g_final, batch, tm, tf):
    n, d = x.shape
    dff = wg.shape[1]
    nj = dff // tf
    bpb = n // batch // tm
    resident = (4 * _nbytes((tm, d), F32) + _nbytes((tm, d), BF16) + _nbytes((tm, d), F32)
                + 6 * _nbytes((d, tf), BF16) + 8 * _nbytes((tm, tf), F32) + 2 * _nbytes((2, dff), F32))
    y, conv = pl.pallas_call(
        functools.partial(_ffn_kernel, tm=tm, blocks_per_batch=bpb),
        grid=(n // tm, nj),
        in_specs=[pl.BlockSpec((tm, d), lambda i, j: (i, 0)),
                  pl.BlockSpec((1, d), lambda i, j: (0, 0)),
                  pl.BlockSpec((d, tf), lambda i, j: (0, j)),
                  pl.BlockSpec((d, tf), lambda i, j: (0, j)),
                  pl.BlockSpec((tf, d), lambda i, j: (j, 0)),
                  pl.BlockSpec((CONV_W, tf), lambda i, j: (0, j)),
                  pl.BlockSpec((1, tf), lambda i, j: (0, j)),
                  pl.BlockSpec((1, CONV_W - 1, tf), lambda i, j: (i // bpb, 0, j)),
                  pl.BlockSpec((1, d), lambda i, j: (0, 0))],
        out_specs=[pl.BlockSpec((tm, d), lambda i, j: (i, 0)),
                   pl.BlockSpec((1, CONV_W - 1, tf), lambda i, j: (i, 0, j))],
        out_shape=[jax.ShapeDtypeStruct((n, d), F32),
                   jax.ShapeDtypeStruct((n // tm, CONV_W - 1, dff), F32)],
        scratch_shapes=[pltpu.VMEM((tm, d), BF16), pltpu.VMEM((tm, d), F32),
                        pltpu.VMEM((nj, CONV_W - 1, tf), F32)],
        compiler_params=pltpu.CompilerParams(
            dimension_semantics=("arbitrary", "arbitrary"),
            vmem_limit_bytes=_vmem_limit(resident)),
        name="conv_ffn",
    )(x, g_ffn, wg, wu, wd, conv_w, conv_b, past, g_final)
    return y, conv[bpb - 1::bpb]


def _pad_cols(a, width):
    return jnp.pad(a, ((0, 0), (0, width - a.shape[1])))


def _layer_pass(x, wts, batch, seq, da_cache, ml_state, conv_past, mem_kv, lam_init, ml_chunk, tiles):
    n, d = x.shape
    tm = tiles["tm"]
    width = wts["da_width"]
    plan = [(0, BF16, DA_QSCALE), (1, F32, None), (1, BF16, None), (2, F32, None), (2, BF16, None),
            (3, BF16, None), (4, BF16, None), (5, BF16, None), (6, BF16, None)]
    (dq, k_f32, dk, v_f32, dv, mq, mk, mv, mo, gates) = _norm_proj(
        x, wts["g_mix"], wts["w_in"], 7, width, plan, tm, tiles["tn_in"],
        gate_w=(wts["w_gate_hi"], wts["w_gate_lo"]), name="in_proj")

    if da_cache is None:
        da_out = _da_prompt(dq, dk, dv, wts["lams"], wts["g_da"], batch, seq, lam_init, tiles["tq"])
    else:
        da_out = _da_sample(dq, dk, dv, da_cache[0], da_cache[1], wts["lams"], wts["g_da"], lam_init,
                            tiles["tk_cache"])

    ml_out, c_new, n_new, m_new = _mlstm(mq, mk, mv, mo, gates, wts["gate_bias"], wts["g_ml"],
                                         ml_state[0], ml_state[1], ml_state[2], batch, ml_chunk)

    x1 = _matmul_res(da_out, 0, ml_out, 0, wts["w_out"], x, tm, tiles["tn"], "out_proj")

    (mq_x,) = _norm_proj(x1, wts["g_xattn"], wts["w_mq"], 1, d, [(0, BF16, None)], tm, tiles["tn"],
                         name="memq_proj")
    mo_x = _mem_attn(mq_x, mem_kv[0], mem_kv[1], batch, min(tm, seq))
    x2 = _matmul_res(mo_x, 0, mo_x, 1, wts["w_mo"], x1, tm, tiles["tn"], "memo_proj")

    y, conv_new = _ffn(x2, wts["g_ffn"], wts["w_ffn_gate"], wts["w_ffn_up"], wts["w_ffn_down"],
                       wts["conv_w"], wts["conv_b"], conv_past, wts["g_final"], batch,
                       tiles["tm_ffn"], tiles["tf"])
    return y, k_f32, v_f32, (c_new, n_new, m_new), conv_new


def kernel(x_prompt, x_sample, cache_da_k, cache_da_v, state_ml_c, state_ml_n, state_ml_m, state_ffn_conv,
           cache_mem_k, cache_mem_v, mem_prompt, w_in, g_mix, lambda_q1, lambda_k1, lambda_q2, lambda_k2,
           g_da_sub, b_ig, b_fg, g_ml, w_out, g_xattn, g_mem, w_mq, w_mk, w_mv, w_mo, g_ffn, w_gate, w_up,
           conv_w, conv_b, w_down, g_final):
    depth = w_in.shape[0]
    assert depth == 1, "single-layer encoder"
    batch, seq, d = x_prompt.shape
    dbatch, dseq, _ = x_sample.shape
    past = cache_da_k.shape[2]
    mlen = mem_prompt.shape[1]
    dff = w_gate.shape[2]
    da_width = d // 2
    ml_width = d - da_width
    hd = ml_width // ML_HEADS
    assert seq % 512 == 0 and past % 1024 == 0 and past % CHUNK == 0 and dseq <= CHUNK
    tf = 512
    dff_p = -(-dff // tf) * tf
    l = 0
    lam_init = 0.8 - 0.6 * math.exp(-0.3 * l)

    n_main = 3 * da_width + 4 * ml_width
    w_in_l = w_in[l]
    w_g = _pad_cols(w_in_l[:, n_main:], LANES)
    w_g_hi = w_g.astype(BF16)
    wts = dict(
        da_width=da_width,
        w_in=w_in_l[:, :n_main].astype(BF16),
        w_gate_hi=w_g_hi,
        w_gate_lo=(w_g - w_g_hi.astype(F32)).astype(BF16),
        gate_bias=_pad_cols(jnp.concatenate([b_ig[l], b_fg[l]])[None, :], LANES),
        g_mix=g_mix[l], g_xattn=g_xattn[l], g_ffn=g_ffn[l].reshape(1, d), g_final=g_final.reshape(1, d),
        lams=jnp.stack([lambda_q1[l], lambda_k1[l], lambda_q2[l], lambda_k2[l]]),
        g_da=g_da_sub[l].reshape(1, 2 * DA_HEAD_DIM),
        g_ml=g_ml[l].reshape(1, ml_width),
        w_out=w_out[l].astype(BF16), w_mq=w_mq[l].astype(BF16), w_mo=w_mo[l].astype(BF16),
        w_ffn_gate=_pad_cols(w_gate[l], dff_p).astype(BF16),
        w_ffn_up=_pad_cols(w_up[l], dff_p).astype(BF16),
        w_ffn_down=jnp.pad(w_down[l], ((0, dff_p - dff), (0, 0))).astype(BF16),
        conv_w=_pad_cols(conv_w[l], dff_p),
        conv_b=_pad_cols(conv_b[l][None, :], dff_p),
    )

    w_mkv = jnp.concatenate([w_mk[l], w_mv[l]], axis=1).astype(BF16)
    mk_f32, mk_bf, mv_f32, mv_bf = _norm_proj(
        mem_prompt.reshape(batch * mlen, d), g_mem[l], w_mkv, 2, d,
        [(0, F32, None), (0, BF16, None), (1, F32, None), (1, BF16, None)], 512, 512, name="memkv_proj")
    zero_state = (jnp.zeros((batch, ML_HEADS, hd, hd), F32), jnp.zeros((batch, ML_HEADS, hd), F32),
                  jnp.zeros((batch, ML_HEADS), F32))
    tiles_p = dict(tm=512, tn_in=256, tn=d, tq=512, tm_ffn=512, tf=tf)
    yp, pk, pv, (pc, pn, pm), pconv = _layer_pass(
        x_prompt.reshape(batch * seq, d), wts, batch, seq, None, zero_state,
        jnp.zeros((batch, CONV_W - 1, dff_p), F32),
        (mk_bf.reshape(batch, mlen, d), mv_bf.reshape(batch, mlen, d)), lam_init, CHUNK, tiles_p)

    ns = dbatch * dseq
    tiles_s = dict(tm=ns, tn_in=256, tn=d, tk_cache=1024, tm_ffn=dseq, tf=tf)
    ys, sk, sv, (sc, sn, sm), sconv = _layer_pass(
        x_sample.reshape(ns, d), wts, dbatch, dseq,
        (cache_da_k, cache_da_v),
        (state_ml_c[l], state_ml_n[l], state_ml_m[l]),
        _pad_cols(state_ffn_conv[l].reshape(dbatch * (CONV_W - 1), dff), dff_p).reshape(dbatch, CONV_W - 1, dff_p),
        (cache_mem_k[l].reshape(dbatch, mlen, d).astype(BF16), cache_mem_v[l].reshape(dbatch, mlen, d).astype(BF16)),
        lam_init, dseq, tiles_s)

    kv_shape_p = (1, batch, seq, DA_HEADS, 2 * DA_HEAD_DIM)
    kv_shape_s = (1, dbatch, dseq, DA_HEADS, 2 * DA_HEAD_DIM)
    mem_shape = (1, batch, mlen, MEM_HEADS, d // MEM_HEADS)
    return (yp.reshape(batch, seq, d), ys.reshape(dbatch, dseq, d),
            pk.reshape(kv_shape_p), pv.reshape(kv_shape_p), pc[None], pn[None], pm[None],
            pconv[None, :, :, :dff], mk_f32.reshape(mem_shape), mv_f32.reshape(mem_shape),
            sk.reshape(kv_shape_s), sv.reshape(kv_shape_s), sc[None], sn[None], sm[None],
            sconv[None, :, :, :dff])
```
